```python
import jax, jax.numpy as jnp
from jax import lax
import numpy as np

D_MODEL = 1024
BATCH = 2
SEQ = 16384
DEPTH = 4

D_MIX = D_MODEL
CONV_CH = D_MIX // 2
HGRN_W = D_MIX - CONV_CH
CONV_GROUPS = 8
CONV_K = 3
HGRN_EXPAND = 128
HGRN_HEADS = HGRN_W // HGRN_EXPAND
HGRN_DK = HGRN_EXPAND
HGRN_DV = HGRN_W // HGRN_HEADS
CHUNK = 64
D_IN_PROJ = 3 * CONV_CH + 4 * HGRN_W
SPLITS = (CONV_CH, 2 * CONV_CH, 3 * CONV_CH, 3 * CONV_CH + HGRN_W, 3 * CONV_CH + 2 * HGRN_W, 3 * CONV_CH + 3 * HGRN_W)
N_EXPERTS = 32
TOP_K = 4
D_EXPERT = D_MODEL
SWIGLU_LIMIT = 7.0
SWIGLU_ALPHA = 1.702
MOE_BLOCK = 256
RMS_EPS = 1e-6

kernel_name = 'hybrid_conv_hgrn2_moe_adaln'


def rms_norm(x, g):
    xf = x.astype(jnp.float32)
    y = xf * lax.rsqrt(jnp.mean(xf * xf, axis=-1, keepdims=True) + RMS_EPS)
    return (y * g.astype(jnp.float32)).astype(x.dtype)


def modulate(xn, shift, scale):
    return xn * (1 + scale[:, None, :]) + shift[:, None, :]


def short_conv(u, w):
    ch = u.shape[-1]
    return lax.conv_general_dilated(u, w.reshape(CONV_K, 1, ch).astype(u.dtype), window_strides=(1,), padding=[(CONV_K - 1, 0)], dimension_numbers=('NWC', 'WIO', 'NWC'), feature_group_count=ch)


def hgrn2_scan(q, k, v, log_f):
    bsz, seq, h, dk = q.shape
    dv = v.shape[-1]
    n = seq // CHUNK

    def to_chunks(t):
        return t.reshape(bsz, n, CHUNK, h, t.shape[-1]).transpose(1, 0, 3, 2, 4)

    causal = jnp.tril(jnp.ones((CHUNK, CHUNK), dtype=bool))

    def step(state, inp):
        qc, kc, vc, gc = inp
        G = jnp.cumsum(gc, axis=2)
        diff = G[:, :, :, None, :] - G[:, :, None, :, :]
        decay = jnp.exp(jnp.where(causal[:, :, None], diff, -jnp.inf))
        scores = jnp.einsum('bhtd,bhsd,bhtsd->bhts', qc, kc, decay)
        o = jnp.einsum('bhts,bhsv->bhtv', scores, vc) + jnp.einsum('bhtd,bhdv->bhtv', qc * jnp.exp(G), state)
        G_last = G[:, :, -1:, :]
        state = jnp.exp(G_last[:, :, 0, :])[..., None] * state + jnp.einsum('bhsd,bhsv->bhdv', kc * jnp.exp(G_last - G), vc)
        return state, o

    init = jnp.zeros((bsz, h, dk, dv), jnp.float32)
    _, o = lax.scan(step, init, (to_chunks(q), to_chunks(k), to_chunks(v), to_chunks(log_f)))
    return o.transpose(1, 0, 3, 2, 4).reshape(bsz, seq, h, dv)


def token_mixer(hn, w_in, conv_w, conv_norm_g, lb, hgrn_norm_g, w_out):
    bsz, seq, _ = hn.shape
    proj = hn @ w_in
    cb, cc, ch, q, f, i, og = jnp.split(proj, SPLITS, axis=-1)
    yc = cb * short_conv(cc * ch, conv_w)
    yc = rms_norm(yc.reshape(bsz, seq, CONV_GROUPS, CONV_CH // CONV_GROUPS), conv_norm_g.reshape(CONV_GROUPS, -1)).reshape(bsz, seq, CONV_CH)
    lbf = lb.astype(jnp.float32)
    qf = jax.nn.silu(q.astype(jnp.float32)).reshape(bsz, seq, HGRN_HEADS, HGRN_DK)
    log_f = jnp.logaddexp(jnp.log(lbf), jnp.log1p(-lbf) + jax.nn.log_sigmoid(f.astype(jnp.float32)))
    log_f = log_f.reshape(bsz, seq, HGRN_HEADS, HGRN_DK)
    kf = -jnp.expm1(log_f)
    vf = i.astype(jnp.float32).reshape(bsz, seq, HGRN_HEADS, HGRN_DV)
    oh = hgrn2_scan(qf, kf, vf, log_f).astype(hn.dtype)
    oh = rms_norm(oh, hgrn_norm_g.reshape(HGRN_HEADS, HGRN_DV)).reshape(bsz, seq, HGRN_W)
    oh = oh * jax.nn.silu(og)
    return jnp.concatenate([yc, oh], axis=-1) @ w_out


def moe(h, w_r, b_r, w_gu, b_gu, w_dn, b_dn):
    bsz, seq, d = h.shape
    n_tok = bsz * seq
    xt = h.reshape(n_tok, d)
    logits = (xt @ w_r + b_r).astype(jnp.float32)
    top_val, top_idx = lax.top_k(logits, TOP_K)
    gates = jax.nn.softmax(top_val, axis=-1).astype(h.dtype)
    n_assign = n_tok * TOP_K
    e_flat = top_idx.reshape(n_assign)
    tok_flat = jnp.broadcast_to(jnp.arange(n_tok, dtype=jnp.int32)[:, None], (n_tok, TOP_K)).reshape(n_assign)
    g_flat = gates.reshape(n_assign)
    order = jnp.argsort(e_flat)
    e_s, tok_s, g_s = e_flat[order], tok_flat[order], g_flat[order]
    counts = jnp.bincount(e_flat, length=N_EXPERTS)
    starts = jnp.cumsum(counts) - counts
    padded = ((counts + MOE_BLOCK - 1) // MOE_BLOCK) * MOE_BLOCK
    pends = jnp.cumsum(padded)
    pstarts = pends - padded
    pos = pstarts[e_s] + jnp.arange(n_assign, dtype=jnp.int32) - starts[e_s]
    n_blocks = -(-n_assign // MOE_BLOCK) + N_EXPERTS
    p_rows = n_blocks * MOE_BLOCK
    pad_tok = jnp.full((p_rows,), n_tok, dtype=jnp.int32).at[pos].set(tok_s)
    pad_gate = jnp.zeros((p_rows,), h.dtype).at[pos].set(g_s)
    block_eid = jnp.minimum(jnp.searchsorted(pends, jnp.arange(n_blocks, dtype=pends.dtype) * MOE_BLOCK, side='right'), N_EXPERTS - 1)
    x_pad = jnp.concatenate([xt, jnp.zeros((1, d), xt.dtype)], axis=0)

    def expert_block(args):
        tok, gate, e = args
        hgu = x_pad[tok] @ w_gu[e] + b_gu[e]
        a, b = jnp.split(hgu, 2, axis=-1)
        a = jnp.minimum(a, SWIGLU_LIMIT)
        b = jnp.clip(b, -SWIGLU_LIMIT, SWIGLU_LIMIT)
        act = a * jax.nn.sigmoid(SWIGLU_ALPHA * a) * (b + 1)
        return (act @ w_dn[e] + b_dn[e]) * gate[:, None]

    yb = lax.map(expert_block, (pad_tok.reshape(n_blocks, MOE_BLOCK), pad_gate.reshape(n_blocks, MOE_BLOCK), block_eid))
    out = jax.ops.segment_sum(yb.reshape(p_rows, d), pad_tok, num_segments=n_tok + 1)[:n_tok]
    return out.reshape(bsz, seq, d)


def setup_inputs(seed: int = 0) -> dict:
    key = jax.random.key(seed)
    ks = jax.random.split(key, 19)
    nrm = jax.random.normal
    D, E, F = D_MODEL, N_EXPERTS, D_EXPERT
    return {
        'x': nrm(ks[0], (BATCH, SEQ, D), jnp.float32),
        'c': nrm(ks[1], (BATCH, D), jnp.float32),
        'w_ada': nrm(ks[2], (DEPTH, D, 6 * D), jnp.float32) * (0.5 * D ** -0.5),
        'b_ada': nrm(ks[3], (DEPTH, 6 * D), jnp.float32) * 0.02,
        'norm1_g': 1.0 + 0.02 * nrm(ks[4], (DEPTH, D), jnp.float32),
        'norm2_g': 1.0 + 0.02 * nrm(ks[5], (DEPTH, D), jnp.float32),
        'w_in': nrm(ks[6], (DEPTH, D, D_IN_PROJ), jnp.float32) * D ** -0.5,
        'conv_w': nrm(ks[7], (DEPTH, CONV_K, CONV_CH), jnp.float32) * CONV_K ** -0.5,
        'conv_norm_g': 1.0 + 0.02 * nrm(ks[8], (DEPTH, CONV_CH), jnp.float32),
        'lower_bounds': 0.5 * nrm(ks[9], (DEPTH, HGRN_W), jnp.float32),
        'hgrn_norm_g': 1.0 + 0.02 * nrm(ks[10], (DEPTH, HGRN_W), jnp.float32),
        'w_out': nrm(ks[11], (DEPTH, D_MIX, D), jnp.float32) * D_MIX ** -0.5,
        'w_router': nrm(ks[12], (DEPTH, D, E), jnp.float32) * D ** -0.5,
        'b_router': nrm(ks[13], (DEPTH, E), jnp.float32) * 0.01,
        'w_gu': nrm(ks[14], (DEPTH, E, D, 2 * F), jnp.float32) * D ** -0.5,
        'b_gu': nrm(ks[15], (DEPTH, E, 2 * F), jnp.float32) * 0.02,
        'w_down': nrm(ks[16], (DEPTH, E, F, D), jnp.float32) * F ** -0.5,
        'b_down': nrm(ks[17], (DEPTH, E, D), jnp.float32) * 0.02,
        'final_g': 1.0 + 0.02 * nrm(ks[18], (D,), jnp.float32),
    }


def reference(x, c, w_ada, b_ada, norm1_g, norm2_g, w_in, conv_w, conv_norm_g, lower_bounds, hgrn_norm_g, w_out, w_router, b_router, w_gu, b_gu, w_down, b_down, final_g):
    lb_all = jnp.cumsum(jax.nn.softmax(lower_bounds.astype(jnp.float32), axis=0), axis=0)
    lb_all = lb_all - lb_all[0]
    c_act = jax.nn.silu(c)
    for l in range(DEPTH):
        ada = c_act @ w_ada[l] + b_ada[l]
        sh1, sc1, g1, sh2, sc2, g2 = jnp.split(ada, 6, axis=-1)
        hn = modulate(rms_norm(x, norm1_g[l]), sh1, sc1)
        x = x + g1[:, None, :] * token_mixer(hn, w_in[l], conv_w[l], conv_norm_g[l], lb_all[l], hgrn_norm_g[l], w_out[l])
        hn = modulate(rms_norm(x, norm2_g[l]), sh2, sc2)
        x = x + g2[:, None, :] * moe(hn, w_router[l], b_router[l], w_gu[l], b_gu[l], w_down[l], b_down[l])
    return rms_norm(x, final_g)
```

```python
import functools

import jax
import jax.numpy as jnp
from jax import lax
from jax.experimental import pallas as pl
from jax.experimental.pallas import tpu as pltpu

F32 = jnp.float32
BF16 = jnp.bfloat16
I32 = jnp.int32
HIGHEST = lax.Precision.HIGHEST

RMS_EPS = 1e-6
N_EXPERTS = 32
TOP_K = 4
CONV_GROUP = 64
HEAD_DIM = 128
SWIGLU_LIMIT = 7.0
SWIGLU_ALPHA = 1.702
SUBLANES = 8
VMEM_LIMIT = 56 * 1024 * 1024

MIX_ROWS = 256
EXPERT_ROWS = 256
DISPATCH_ROWS = 512
ADA_COLS = 1536


def _dot(a, b):
    return jnp.dot(a, b, preferred_element_type=F32)


def _dot_nt(a, b, precision=None):
    return lax.dot_general(a, b, (((1,), (1,)), ((), ())), precision=precision,
                           preferred_element_type=F32)


def _dot_tn(a, b):
    return lax.dot_general(a, b, (((0,), (0,)), ((), ())), preferred_element_type=F32)


def _sigmoid(x):
    return 1.0 / (1.0 + jnp.exp(-x))


def _rms(x):
    return x * lax.rsqrt(jnp.mean(x * x, axis=-1, keepdims=True) + RMS_EPS)


def _pack_bf16_pair(lo, hi):
    lo_bits = lax.bitcast_convert_type(lo.astype(BF16).astype(F32), I32)
    hi_bits = lax.bitcast_convert_type(hi.astype(BF16).astype(F32), I32)
    return lax.shift_right_logical(lo_bits, 16) | (hi_bits & jnp.int32(-65536))


def _unpack_bf16_pair(p):
    lo = lax.bitcast_convert_type(lax.shift_left(p, 16), F32)
    hi = lax.bitcast_convert_type(p & jnp.int32(-65536), F32)
    return lo, hi


def _ada_kernel(c_ref, w_ref, b_ref, o_ref):
    cv = c_ref[...]
    ca = cv * _sigmoid(cv)
    o_ref[0] = jnp.dot(ca, w_ref[0], precision=HIGHEST, preferred_element_type=F32) + b_ref[0]


def _ada_call(c_pad, w_ada, b_ada):
    depth, d, n6 = w_ada.shape
    rows = c_pad.shape[0]
    return pl.pallas_call(
        _ada_kernel,
        grid=(depth, n6 // ADA_COLS),
        in_specs=[
            pl.BlockSpec((rows, d), lambda l, j: (0, 0)),
            pl.BlockSpec((1, d, ADA_COLS), lambda l, j: (l, 0, j)),
            pl.BlockSpec((1, 1, ADA_COLS), lambda l, j: (l, 0, j)),
        ],
        out_specs=pl.BlockSpec((1, rows, ADA_COLS), lambda l, j: (l, 0, j)),
        out_shape=jax.ShapeDtypeStruct((depth, rows, n6), F32),
        compiler_params=pltpu.CompilerParams(
            dimension_semantics=("arbitrary", "arbitrary"), vmem_limit_bytes=VMEM_LIMIT),
        name="ada_proj",
    )(c_pad, w_ada, b_ada.reshape(depth, 1, n6))


def _bounds_kernel(lb_ref, o_ref):
    rows = [lb_ref[l:l + 1, :] for l in range(lb_ref.shape[0])]
    m = functools.reduce(jnp.maximum, rows)
    es = [jnp.exp(r - m) for r in rows]
    tot = functools.reduce(lambda a, b: a + b, es)
    cum = None
    first = None
    for l, e in enumerate(es):
        p = e / tot
        cum = p if cum is None else cum + p
        if first is None:
            first = cum
        o_ref[l:l + 1, :] = cum - first


def _bounds_call(lower_bounds):
    return pl.pallas_call(
        _bounds_kernel,
        out_shape=jax.ShapeDtypeStruct(lower_bounds.shape, F32),
        name="hgrn_bounds",
    )(lower_bounds.astype(F32))


def _hgrn_head(q, k, v, g_cum, st_ref, h, row):
    t_rows = q.shape[0]
    col = lax.broadcasted_iota(I32, (t_rows, t_rows), 1)
    rowsq = lax.broadcasted_iota(I32, (t_rows, t_rows), 0)
    scores = jnp.zeros((t_rows, t_rows), F32)
    half = t_rows // 2
    while half >= SUBLANES:
        grp = 2 * half
        g_mid = g_cum.reshape(t_rows // grp, grp, HEAD_DIM)[:, half - 1:half, :]
        g_mid = jnp.broadcast_to(g_mid, (t_rows // grp, grp, HEAD_DIM)).reshape(t_rows, HEAD_DIM)
        upper = (row & half) != 0
        d = g_cum - g_mid
        e = jnp.exp(jnp.where(upper, d, -d))
        qh = jnp.where(upper, q * e, 0.0).astype(BF16)
        kh = jnp.where(upper, 0.0, k * e).astype(BF16)
        sc = _dot_nt(qh, kh)
        if grp < t_rows:
            shift = grp.bit_length() - 1
            sc = jnp.where(lax.shift_right_logical(rowsq, shift) == lax.shift_right_logical(col, shift),
                           sc, 0.0)
        scores = scores + sc
        half //= 2

    out = _dot(scores.astype(BF16), v.astype(BF16))
    sub = row & (SUBLANES - 1)
    for delta in range(SUBLANES):
        if delta == 0:
            w = q * k
            vr = v
        else:
            kr = pltpu.roll(k, delta, 0)
            gr = pltpu.roll(g_cum, delta, 0)
            vr = pltpu.roll(v, delta, 0)
            valid = sub >= delta
            w = jnp.where(valid, q * kr * jnp.exp(jnp.where(valid, g_cum - gr, 0.0)), 0.0)
        out = out + jnp.sum(w, axis=-1, keepdims=True) * vr

    st = st_ref[h]
    out = out + _dot_nt((q * jnp.exp(g_cum)).astype(BF16), st.astype(BF16))
    g_last = g_cum[t_rows - 1:t_rows, :]
    kd = (k * jnp.exp(g_last - g_cum)).astype(BF16)
    st_ref[h] = st * jnp.exp(g_last) + _dot_tn(v.astype(BF16), kd)
    return out


def _mixer_kernel(x_ref, ada_ref, n1g_ref, win_ref, convw_ref, cng_ref, lb_ref, hng_ref, wout_ref,
                  gmean_ref, n2g_ref, wrt_ref, br_ref, upper_ref,
                  xo_ref, hn2_ref, eidx_ref, gate_ref, rank_ref, cnt_ref,
                  st_ref, carry_ref, base_ref, *, conv_ch, hgrn_w):
    b = pl.program_id(0)
    i = pl.program_id(1)
    t_rows = x_ref.shape[1]
    heads = hgrn_w // HEAD_DIM

    @pl.when(i == 0)
    def _():
        st_ref[...] = jnp.zeros_like(st_ref)
        carry_ref[...] = jnp.zeros_like(carry_ref)

    @pl.when((b == 0) & (i == 0))
    def _():
        base_ref[...] = jnp.zeros_like(base_ref)

    sh1, sc1, g1 = ada_ref[0, 0:1, :], ada_ref[0, 1:2, :], ada_ref[0, 2:3, :]
    sh2, sc2, g2 = ada_ref[0, 3:4, :], ada_ref[0, 4:5, :], ada_ref[0, 5:6, :]
    del g2

    x = x_ref[0]
    hn = (_rms(x) * n1g_ref[...]) * (1.0 + sc1) + sh1
    proj = _dot(hn.astype(BF16), win_ref[...])
    c0 = conv_ch
    cb, cc, ch = proj[:, 0:c0], proj[:, c0:2 * c0], proj[:, 2 * c0:3 * c0]
    o0 = 3 * c0
    q_raw = proj[:, o0:o0 + hgrn_w]
    f_raw = proj[:, o0 + hgrn_w:o0 + 2 * hgrn_w]
    v_all = proj[:, o0 + 2 * hgrn_w:o0 + 3 * hgrn_w]
    og = proj[:, o0 + 3 * hgrn_w:o0 + 4 * hgrn_w]

    u = cc * ch
    carry = carry_ref[...]
    row8 = lax.broadcasted_iota(I32, (SUBLANES, c0), 0)
    u1 = pltpu.roll(u, 1, 0)
    u2 = pltpu.roll(u, 2, 0)
    u1 = jnp.concatenate([jnp.where(row8 < 1, pltpu.roll(carry, 1, 0), u1[0:SUBLANES]), u1[SUBLANES:]], axis=0)
    u2 = jnp.concatenate([jnp.where(row8 < 2, pltpu.roll(carry, 2, 0), u2[0:SUBLANES]), u2[SUBLANES:]], axis=0)
    carry_ref[...] = u[t_rows - SUBLANES:t_rows]
    yc = cb * (convw_ref[0:1, :] * u2 + convw_ref[1:2, :] * u1 + convw_ref[2:3, :] * u)
    sq = yc * yc
    sq_hi = sq.astype(BF16)
    sq_lo = (sq - sq_hi.astype(F32)).astype(BF16)
    gms = _dot(sq_hi, gmean_ref[...]) + _dot(sq_lo, gmean_ref[...])
    ycn = yc * lax.rsqrt(gms + RMS_EPS) * cng_ref[...]

    lb = lb_ref[...]
    log_lb = jnp.log(lb)
    log_1mlb = jnp.log1p(-lb)
    log_sig = jnp.minimum(f_raw, 0.0) - jnp.log1p(jnp.exp(-jnp.abs(f_raw)))
    bb = log_1mlb + log_sig
    log_f = jnp.maximum(log_lb, bb) + jnp.log1p(jnp.exp(-jnp.abs(log_lb - bb)))
    k_all = 1.0 - jnp.exp(log_f)
    q_all = q_raw * _sigmoid(q_raw)
    roww = lax.broadcasted_iota(I32, (t_rows, hgrn_w), 0)
    g_cum = log_f
    s = 1
    while s < t_rows:
        g_cum = g_cum + jnp.where(roww >= s, pltpu.roll(g_cum, s, 0), 0.0)
        s *= 2
    row = lax.broadcasted_iota(I32, (t_rows, HEAD_DIM), 0)
    outs = []
    for h in range(heads):
        sl = slice(h * HEAD_DIM, (h + 1) * HEAD_DIM)
        o = _hgrn_head(q_all[:, sl], k_all[:, sl], v_all[:, sl], g_cum[:, sl], st_ref, h, row)
        outs.append(_rms(o))
    oh = jnp.concatenate(outs, axis=1) * hng_ref[...] * (og * _sigmoid(og))

    mix = _dot(ycn.astype(BF16), wout_ref[0:c0, :]) + _dot(oh.astype(BF16), wout_ref[c0:c0 + hgrn_w, :])
    xn = x + g1 * mix
    xo_ref[0] = xn

    hn2 = (_rms(xn) * n2g_ref[...]) * (1.0 + sc2) + sh2
    dh = hn2.shape[1] // 2
    hn2_ref[...] = _pack_bf16_pair(hn2[:, 0:dh], hn2[:, dh:2 * dh])
    logits = _dot_nt(wrt_ref[...], hn2, precision=HIGHEST) + br_ref[:, 0:1]
    eio = lax.broadcasted_iota(I32, logits.shape, 0).astype(F32)
    vals, sels = [], []
    work = logits
    for kk in range(TOP_K):
        m = jnp.max(work, axis=0, keepdims=True)
        idx = jnp.min(jnp.where(work == m, eio, float(N_EXPERTS)), axis=0, keepdims=True)
        sel = eio == idx
        vals.append(m)
        sels.append(sel)
        eidx_ref[kk:kk + 1, :] = idx.astype(I32)
        work = jnp.where(sel, -jnp.inf, work)
    exps = [jnp.exp(vv - vals[0]) for vv in vals]
    tot = functools.reduce(lambda a, c: a + c, exps)
    member = functools.reduce(lambda a, c: a + c, [s_.astype(F32) for s_ in sels])
    before = _dot(member.astype(BF16), upper_ref[...])
    base = base_ref[:, 0:1]
    slot = before + base
    for kk in range(TOP_K):
        gate_ref[kk:kk + 1, :] = exps[kk] / tot
        rank_ref[kk:kk + 1, :] = jnp.sum(jnp.where(sels[kk], slot, 0.0), axis=0, keepdims=True).astype(I32)
    base_new = base + jnp.sum(member, axis=1, keepdims=True)
    base_ref[...] = jnp.broadcast_to(base_new, base_ref.shape)
    cnt_ref[...] = jnp.broadcast_to(base_new, cnt_ref.shape)


def _mixer_call(x, ada_l, n1g, win, convw, cng, lb, hng, wout, gmean, n2g, wrt, br, upper):
    bsz, seq, d = x.shape
    t = MIX_ROWS
    n_tok = bsz * seq
    conv_ch = convw.shape[1]
    hgrn_w = lb.shape[1]
    heads = hgrn_w // HEAD_DIM
    steps = seq // t
    full = lambda a: pl.BlockSpec(a.shape, lambda b, i: (0,) * a.ndim)
    tokmap = lambda b, i: (0, b * steps + i)
    kern = functools.partial(_mixer_kernel, conv_ch=conv_ch, hgrn_w=hgrn_w)
    return pl.pallas_call(
        kern,
        grid=(bsz, steps),
        in_specs=[
            pl.BlockSpec((1, t, d), lambda b, i: (b, i, 0)),
            pl.BlockSpec((1, 6, d), lambda b, i: (b, 0, 0)),
            full(n1g), full(win), full(convw), full(cng), full(lb), full(hng), full(wout),
            full(gmean), full(n2g), full(wrt), full(br), full(upper),
        ],
        out_specs=[
            pl.BlockSpec((1, t, d), lambda b, i: (b, i, 0)),
            pl.BlockSpec((t, d // 2), lambda b, i: (b * steps + i, 0)),
            pl.BlockSpec((TOP_K, t), tokmap),
            pl.BlockSpec((TOP_K, t), tokmap),
            pl.BlockSpec((TOP_K, t), tokmap),
            pl.BlockSpec((N_EXPERTS, 128), lambda b, i: (0, 0)),
        ],
        out_shape=[
            jax.ShapeDtypeStruct((bsz, seq, d), F32),
            jax.ShapeDtypeStruct((n_tok, d // 2), I32),
            jax.ShapeDtypeStruct((TOP_K, n_tok), I32),
            jax.ShapeDtypeStruct((TOP_K, n_tok), F32),
            jax.ShapeDtypeStruct((TOP_K, n_tok), I32),
            jax.ShapeDtypeStruct((N_EXPERTS, 128), F32),
        ],
        scratch_shapes=[
            pltpu.VMEM((heads, HEAD_DIM, HEAD_DIM), F32),
            pltpu.VMEM((SUBLANES, conv_ch), F32),
            pltpu.VMEM((N_EXPERTS, 128), F32),
        ],
        compiler_params=pltpu.CompilerParams(
            dimension_semantics=("arbitrary", "arbitrary"), vmem_limit_bytes=VMEM_LIMIT),
        name="token_mixer",
    )(x, ada_l, n1g, win, convw, cng, lb, hng, wout, gmean, n2g, wrt, br, upper)


def _dispatch_kernel(pos_ref, src_ref, xs_in_ref, xs_ref, sem):
    del xs_in_ref
    t_rows = src_ref.shape[0]

    def row_copy(t, dst_row):
        return pltpu.make_async_copy(src_ref.at[pl.ds(t, 1)], xs_ref.at[pl.ds(dst_row, 1)], sem)

    def start(t, carry):
        for kk in range(TOP_K):
            row_copy(t, pos_ref[kk, t]).start()
        return carry

    def wait(t, carry):
        for kk in range(TOP_K):
            row_copy(t, pos_ref[kk, t]).wait()
        return carry

    lax.fori_loop(0, t_rows, start, 0)
    lax.fori_loop(0, t_rows, wait, 0)


def _dispatch_call(hn2p, pos, xs_init):
    n_tok, dh = hn2p.shape
    t = DISPATCH_ROWS
    return pl.pallas_call(
        _dispatch_kernel,
        grid=(n_tok // t,),
        in_specs=[
            pl.BlockSpec((TOP_K, t), lambda i: (0, i), memory_space=pltpu.SMEM),
            pl.BlockSpec((t, dh), lambda i: (i, 0)),
            pl.BlockSpec(memory_space=pl.ANY),
        ],
        out_specs=pl.BlockSpec(memory_space=pl.ANY),
        out_shape=jax.ShapeDtypeStruct(xs_init.shape, I32),
        scratch_shapes=[pltpu.SemaphoreType.DMA(())],
        input_output_aliases={2: 0},
        compiler_params=pltpu.CompilerParams(
            dimension_semantics=("arbitrary",), vmem_limit_bytes=VMEM_LIMIT, has_side_effects=True),
        name="moe_dispatch",
    )(pos, hn2p, xs_init)


def _expert_kernel(eid_ref, nblk_ref, xs_ref, wgu_ref, bgu_ref, wdn_ref, bdn_ref, ys_ref):
    del eid_ref
    i = pl.program_id(0)

    @pl.when(i < nblk_ref[0])
    def _():
        lo, hi = _unpack_bf16_pair(xs_ref[...])
        dh = lo.shape[1]
        hgu = (_dot(lo.astype(BF16), wgu_ref[0, 0:dh, :]) + _dot(hi.astype(BF16), wgu_ref[0, dh:2 * dh, :])
               + bgu_ref[0])
        f = hgu.shape[1] // 2
        a = jnp.minimum(hgu[:, 0:f], SWIGLU_LIMIT)
        g = jnp.clip(hgu[:, f:2 * f], -SWIGLU_LIMIT, SWIGLU_LIMIT)
        act = a * _sigmoid(SWIGLU_ALPHA * a) * (g + 1.0)
        y = _dot(act.astype(BF16), wdn_ref[0]) + bdn_ref[0]
        do = y.shape[1] // 2
        ys_ref[...] = _pack_bf16_pair(y[:, 0:do], y[:, do:2 * do])

    @pl.when(i >= nblk_ref[0])
    def _():
        ys_ref[...] = jnp.zeros_like(ys_ref)


def _expert_call(block_eid, n_used, xs, wgu, bgu, wdn, bdn):
    p_rows, dh = xs.shape
    bm = EXPERT_ROWS
    n_e, d, f2 = wgu.shape
    dout = wdn.shape[2]
    grid_spec = pltpu.PrefetchScalarGridSpec(
        num_scalar_prefetch=2,
        grid=(p_rows // bm,),
        in_specs=[
            pl.BlockSpec((bm, dh), lambda i, eid, nb: (i, 0)),
            pl.BlockSpec((1, d, f2), lambda i, eid, nb: (eid[i], 0, 0)),
            pl.BlockSpec((1, 1, f2), lambda i, eid, nb: (eid[i], 0, 0)),
            pl.BlockSpec((1, f2 // 2, dout), lambda i, eid, nb: (eid[i], 0, 0)),
            pl.BlockSpec((1, 1, dout), lambda i, eid, nb: (eid[i], 0, 0)),
        ],
        out_specs=pl.BlockSpec((bm, dout // 2), lambda i, eid, nb: (i, 0)),
    )
    return pl.pallas_call(
        _expert_kernel,
        grid_spec=grid_spec,
        out_shape=jax.ShapeDtypeStruct((p_rows, dout // 2), I32),
        compiler_params=pltpu.CompilerParams(
            dimension_semantics=("arbitrary",), vmem_limit_bytes=VMEM_LIMIT),
        name="moe_experts",
    )(block_eid, n_used, xs, wgu, bgu.reshape(n_e, 1, f2), wdn, bdn.reshape(n_e, 1, dout))


def _combine_kernel(pos_ref, x_ref, gate_ref, g2_ref, fg_ref, ys_ref, o_ref, buf_ref, sem, *, final_norm):
    t_rows = x_ref.shape[0]

    def row_copy(kk, t, src_row):
        return pltpu.make_async_copy(ys_ref.at[pl.ds(src_row, 1)], buf_ref.at[kk, pl.ds(t, 1)], sem)

    def start(t, carry):
        for kk in range(TOP_K):
            row_copy(kk, t, pos_ref[kk, t]).start()
        return carry

    def wait(t, carry):
        for kk in range(TOP_K):
            row_copy(kk, t, pos_ref[kk, t]).wait()
        return carry

    lax.fori_loop(0, t_rows, start, 0)
    lax.fori_loop(0, t_rows, wait, 0)

    dh = buf_ref.shape[2]
    acc_lo = jnp.zeros((t_rows, dh), F32)
    acc_hi = jnp.zeros((t_rows, dh), F32)
    for kk in range(TOP_K):
        lo, hi = _unpack_bf16_pair(buf_ref[kk])
        gk = gate_ref[:, kk:kk + 1]
        acc_lo = acc_lo + gk * lo
        acc_hi = acc_hi + gk * hi
    x = x_ref[...]
    g2 = g2_ref[0]
    out = x + g2 * jnp.concatenate([acc_lo, acc_hi], axis=1)
    if final_norm:
        out = _rms(out) * fg_ref[...]
    o_ref[...] = out


def _combine_call(x2d, gates_t, pos, g2, final_g, ys, seq, final_norm):
    n_tok, d = x2d.shape
    t = DISPATCH_ROWS
    steps = seq // t
    kern = functools.partial(_combine_kernel, final_norm=final_norm)
    return pl.pallas_call(
        kern,
        grid=(n_tok // t,),
        in_specs=[
            pl.BlockSpec((TOP_K, t), lambda i: (0, i), memory_space=pltpu.SMEM),
            pl.BlockSpec((t, d), lambda i: (i, 0)),
            pl.BlockSpec((t, TOP_K), lambda i: (i, 0)),
            pl.BlockSpec((1, 1, d), lambda i: (i // steps, 0, 0)),
            pl.BlockSpec((1, d), lambda i: (0, 0)),
            pl.BlockSpec(memory_space=pl.ANY),
        ],
        out_specs=pl.BlockSpec((t, d), lambda i: (i, 0)),
        out_shape=jax.ShapeDtypeStruct((n_tok, d), F32),
        scratch_shapes=[pltpu.VMEM((TOP_K, t, d // 2), I32), pltpu.SemaphoreType.DMA(())],
        compiler_params=pltpu.CompilerParams(
            dimension_semantics=("arbitrary",), vmem_limit_bytes=VMEM_LIMIT),
        name="moe_combine",
    )(pos, x2d, gates_t, g2, final_g, ys)


def _route_tables(eidx, rank, cnt, n_blocks):
    bm = EXPERT_ROWS
    counts = cnt[:, 0].astype(I32)
    padded = ((counts + bm - 1) // bm) * bm
    pends = jnp.cumsum(padded)
    pstarts = pends - padded
    pos = pstarts[eidx] + rank
    block_eid = jnp.minimum(
        jnp.searchsorted(pends, jnp.arange(n_blocks, dtype=I32) * bm, side="right"), N_EXPERTS - 1).astype(I32)
    n_used = (pends[-1:] // bm).astype(I32)
    return pos, block_eid, n_used


def _moe_layer(xn, hn2p, eidx, gates, rank, cnt, g2, wgu, bgu, wdn, bdn, final_g, final_norm):
    bsz, seq, d = xn.shape
    n_tok = bsz * seq
    n_blocks = n_tok * TOP_K // EXPERT_ROWS + N_EXPERTS
    pos, block_eid, n_used = _route_tables(eidx, rank, cnt, n_blocks)
    xs = _dispatch_call(hn2p, pos, jnp.zeros((n_blocks * EXPERT_ROWS, d // 2), I32))
    ys = _expert_call(block_eid, n_used, xs, wgu, bgu, wdn, bdn)
    out = _combine_call(xn.reshape(n_tok, d), gates.T, pos, g2, final_g, ys, seq, final_norm)
    return out.reshape(bsz, seq, d)


def kernel(x, c, w_ada, b_ada, norm1_g, norm2_g, w_in, conv_w, conv_norm_g, lower_bounds, hgrn_norm_g, w_out,
           w_router, b_router, w_gu, b_gu, w_down, b_down, final_g):
    depth = w_ada.shape[0]
    bsz, seq, d = x.shape
    conv_ch = conv_w.shape[2]
    c_pad = jnp.zeros((SUBLANES, d), F32).at[:bsz].set(c)
    ada = _ada_call(c_pad, w_ada, b_ada)[:, :bsz].reshape(depth, bsz, 6, d)
    lb_all = _bounds_call(lower_bounds)

    ci = jnp.arange(conv_ch)
    gmean = jnp.where((ci[:, None] // CONV_GROUP) == (ci[None, :] // CONV_GROUP), 1.0 / CONV_GROUP, 0.0).astype(BF16)
    ti = jnp.arange(MIX_ROWS)
    upper = (ti[:, None] < ti[None, :]).astype(BF16)

    for l in range(depth):
        xn, hn2p, eidx, gates, rank, cnt = _mixer_call(
            x, ada[l], norm1_g[l][None], w_in[l].astype(BF16), conv_w[l], conv_norm_g[l][None], lb_all[l][None],
            hgrn_norm_g[l][None], w_out[l].astype(BF16), gmean, norm2_g[l][None], w_router[l].T,
            jnp.broadcast_to(b_router[l][:, None], (N_EXPERTS, 128)), upper)
        x = _moe_layer(xn, hn2p, eidx, gates, rank, cnt, ada[l][:, 5:6, :], w_gu[l].astype(BF16), b_gu[l],
                       w_down[l].astype(BF16), b_down[l], final_g[None], final_norm=(l == depth - 1))
    return x
```

```python
import functools

import jax
import jax.numpy as jnp
from jax import lax
from jax.experimental import pallas as pl
from jax.experimental.pallas import tpu as pltpu

F32 = jnp.float32
BF16 = jnp.bfloat16
I32 = jnp.int32
HIGHEST = lax.Precision.HIGHEST

RMS_EPS = 1e-6
N_EXPERTS = 32
TOP_K = 4
CONV_GROUP = 64
HEAD_DIM = 128
SWIGLU_LIMIT = 7.0
SWIGLU_ALPHA = 1.702
SUBLANES = 8
VMEM_LIMIT = 56 * 1024 * 1024

MIX_ROWS = 256
EXPERT_ROWS = 256
DISPATCH_ROWS = 512
ADA_COLS = 1536
WEIGHT_CAST_ROWS = 128


def _dot(a, b):
    return jnp.dot(a, b, preferred_element_type=F32)


def _dot_nt(a, b, precision=None):
    return lax.dot_general(a, b, (((1,), (1,)), ((), ())), precision=precision,
                           preferred_element_type=F32)


def _dot_tn(a, b):
    return lax.dot_general(a, b, (((0,), (0,)), ((), ())), preferred_element_type=F32)


def _sigmoid(x):
    return 1.0 / (1.0 + jnp.exp(-x))


def _rms(x):
    return x * lax.rsqrt(jnp.mean(x * x, axis=-1, keepdims=True) + RMS_EPS)


def _pack_bf16_pair(lo, hi):
    lo_bits = lax.bitcast_convert_type(lo.astype(BF16).astype(F32), I32)
    hi_bits = lax.bitcast_convert_type(hi.astype(BF16).astype(F32), I32)
    return lax.shift_right_logical(lo_bits, 16) | (hi_bits & jnp.int32(-65536))


def _unpack_bf16_pair(p):
    lo = lax.bitcast_convert_type(lax.shift_left(p, 16), F32)
    hi = lax.bitcast_convert_type(p & jnp.int32(-65536), F32)
    return lo, hi


def _ada_kernel(c_ref, w_ref, b_ref, o_ref):
    cv = c_ref[...]
    ca = cv * _sigmoid(cv)
    o_ref[0] = jnp.dot(ca, w_ref[0], precision=HIGHEST, preferred_element_type=F32) + b_ref[0]


def _ada_call(c_pad, w_ada, b_ada):
    depth, d, n6 = w_ada.shape
    rows = c_pad.shape[0]
    return pl.pallas_call(
        _ada_kernel,
        grid=(depth, n6 // ADA_COLS),
        in_specs=[
            pl.BlockSpec((rows, d), lambda l, j: (0, 0)),
            pl.BlockSpec((1, d, ADA_COLS), lambda l, j: (l, 0, j)),
            pl.BlockSpec((1, 1, ADA_COLS), lambda l, j: (l, 0, j)),
        ],
        out_specs=pl.BlockSpec((1, rows, ADA_COLS), lambda l, j: (l, 0, j)),
        out_shape=jax.ShapeDtypeStruct((depth, rows, n6), F32),
        compiler_params=pltpu.CompilerParams(
            dimension_semantics=("arbitrary", "arbitrary"), vmem_limit_bytes=VMEM_LIMIT),
        name="ada_proj",
    )(c_pad, w_ada, b_ada.reshape(depth, 1, n6))


def _bounds_kernel(lb_ref, o_ref):
    rows = [lb_ref[l:l + 1, :] for l in range(lb_ref.shape[0])]
    m = functools.reduce(jnp.maximum, rows)
    es = [jnp.exp(r - m) for r in rows]
    tot = functools.reduce(lambda a, b: a + b, es)
    cum = None
    first = None
    for l, e in enumerate(es):
        p = e / tot
        cum = p if cum is None else cum + p
        if first is None:
            first = cum
        o_ref[l:l + 1, :] = cum - first


def _bounds_call(lower_bounds):
    return pl.pallas_call(
        _bounds_kernel,
        out_shape=jax.ShapeDtypeStruct(lower_bounds.shape, F32),
        name="hgrn_bounds",
    )(lower_bounds.astype(F32))


def _hgrn_head(q, k, v, g_cum, st_ref, h, row):
    t_rows = q.shape[0]
    col = lax.broadcasted_iota(I32, (t_rows, t_rows), 1)
    rowsq = lax.broadcasted_iota(I32, (t_rows, t_rows), 0)
    scores = jnp.zeros((t_rows, t_rows), F32)
    half = t_rows // 2
    while half >= SUBLANES:
        grp = 2 * half
        g_mid = g_cum.reshape(t_rows // grp, grp, HEAD_DIM)[:, half - 1:half, :]
        g_mid = jnp.broadcast_to(g_mid, (t_rows // grp, grp, HEAD_DIM)).reshape(t_rows, HEAD_DIM)
        upper = (row & half) != 0
        d = g_cum - g_mid
        e = jnp.exp(jnp.where(upper, d, -d))
        qh = jnp.where(upper, q * e, 0.0).astype(BF16)
        kh = jnp.where(upper, 0.0, k * e).astype(BF16)
        sc = _dot_nt(qh, kh)
        if grp < t_rows:
            shift = grp.bit_length() - 1
            sc = jnp.where(lax.shift_right_logical(rowsq, shift) == lax.shift_right_logical(col, shift),
                           sc, 0.0)
        scores = scores + sc
        half //= 2

    out = _dot(scores.astype(BF16), v.astype(BF16))
    sub = row & (SUBLANES - 1)
    for delta in range(SUBLANES):
        if delta == 0:
            w = q * k
            vr = v
        else:
            kr = pltpu.roll(k, delta, 0)
            gr = pltpu.roll(g_cum, delta, 0)
            vr = pltpu.roll(v, delta, 0)
            valid = sub >= delta
            w = jnp.where(valid, q * kr * jnp.exp(jnp.where(valid, g_cum - gr, 0.0)), 0.0)
        out = out + jnp.sum(w, axis=-1, keepdims=True) * vr

    st = st_ref[h]
    out = out + _dot_nt((q * jnp.exp(g_cum)).astype(BF16), st.astype(BF16))
    g_last = g_cum[t_rows - 1:t_rows, :]
    kd = (k * jnp.exp(g_last - g_cum)).astype(BF16)
    st_ref[h] = st * jnp.exp(g_last) + _dot_tn(v.astype(BF16), kd)
    return out


def _mixer_kernel(x_ref, ada_ref, n1g_ref, win_ref, convw_ref, cng_ref, lb_ref, hng_ref, wout_ref,
                  gmean_ref, n2g_ref, wrt_ref, br_ref, upper_ref,
                  xo_ref, hn2_ref, eidx_ref, gate_ref, rank_ref, cnt_ref,
                  st_ref, carry_ref, base_ref, *, conv_ch, hgrn_w):
    b = pl.program_id(0)
    i = pl.program_id(1)
    t_rows = x_ref.shape[1]
    heads = hgrn_w // HEAD_DIM

    @pl.when(i == 0)
    def _():
        st_ref[...] = jnp.zeros_like(st_ref)
        carry_ref[...] = jnp.zeros_like(carry_ref)

    @pl.when((b == 0) & (i == 0))
    def _():
        base_ref[...] = jnp.zeros_like(base_ref)

    sh1, sc1, g1 = ada_ref[0, 0:1, :], ada_ref[0, 1:2, :], ada_ref[0, 2:3, :]
    sh2, sc2, g2 = ada_ref[0, 3:4, :], ada_ref[0, 4:5, :], ada_ref[0, 5:6, :]
    del g2

    x = x_ref[0]
    hn = (_rms(x) * n1g_ref[...]) * (1.0 + sc1) + sh1
    proj = _dot(hn.astype(BF16), win_ref[...])
    c0 = conv_ch
    cb, cc, ch = proj[:, 0:c0], proj[:, c0:2 * c0], proj[:, 2 * c0:3 * c0]
    o0 = 3 * c0
    q_raw = proj[:, o0:o0 + hgrn_w]
    f_raw = proj[:, o0 + hgrn_w:o0 + 2 * hgrn_w]
    v_all = proj[:, o0 + 2 * hgrn_w:o0 + 3 * hgrn_w]
    og = proj[:, o0 + 3 * hgrn_w:o0 + 4 * hgrn_w]

    u = cc * ch
    carry = carry_ref[...]
    row8 = lax.broadcasted_iota(I32, (SUBLANES, c0), 0)
    u1 = pltpu.roll(u, 1, 0)
    u2 = pltpu.roll(u, 2, 0)
    u1 = jnp.concatenate([jnp.where(row8 < 1, pltpu.roll(carry, 1, 0), u1[0:SUBLANES]), u1[SUBLANES:]], axis=0)
    u2 = jnp.concatenate([jnp.where(row8 < 2, pltpu.roll(carry, 2, 0), u2[0:SUBLANES]), u2[SUBLANES:]], axis=0)
    carry_ref[...] = u[t_rows - SUBLANES:t_rows]
    yc = cb * (convw_ref[0:1, :] * u2 + convw_ref[1:2, :] * u1 + convw_ref[2:3, :] * u)
    sq = yc * yc
    sq_hi = sq.astype(BF16)
    sq_lo = (sq - sq_hi.astype(F32)).astype(BF16)
    gms = _dot(sq_hi, gmean_ref[...]) + _dot(sq_lo, gmean_ref[...])
    ycn = yc * lax.rsqrt(gms + RMS_EPS) * cng_ref[...]

    lb = lb_ref[...]
    log_lb = jnp.log(lb)
    log_1mlb = jnp.log1p(-lb)
    log_sig = jnp.minimum(f_raw, 0.0) - jnp.log1p(jnp.exp(-jnp.abs(f_raw)))
    bb = log_1mlb + log_sig
    log_f = jnp.maximum(log_lb, bb) + jnp.log1p(jnp.exp(-jnp.abs(log_lb - bb)))
    k_all = 1.0 - jnp.exp(log_f)
    q_all = q_raw * _sigmoid(q_raw)
    roww = lax.broadcasted_iota(I32, (t_rows, hgrn_w), 0)
    g_cum = log_f
    s = 1
    while s < t_rows:
        g_cum = g_cum + jnp.where(roww >= s, pltpu.roll(g_cum, s, 0), 0.0)
        s *= 2
    row = lax.broadcasted_iota(I32, (t_rows, HEAD_DIM), 0)
    outs = []
    for h in range(heads):
        sl = slice(h * HEAD_DIM, (h + 1) * HEAD_DIM)
        o = _hgrn_head(q_all[:, sl], k_all[:, sl], v_all[:, sl], g_cum[:, sl], st_ref, h, row)
        outs.append(_rms(o))
    oh = jnp.concatenate(outs, axis=1) * hng_ref[...] * (og * _sigmoid(og))

    mix = _dot(ycn.astype(BF16), wout_ref[0:c0, :]) + _dot(oh.astype(BF16), wout_ref[c0:c0 + hgrn_w, :])
    xn = x + g1 * mix
    xo_ref[0] = xn

    hn2 = (_rms(xn) * n2g_ref[...]) * (1.0 + sc2) + sh2
    dh = hn2.shape[1] // 2
    hn2_ref[...] = _pack_bf16_pair(hn2[:, 0:dh], hn2[:, dh:2 * dh])
    logits = _dot_nt(wrt_ref[...], hn2, precision=HIGHEST) + br_ref[:, 0:1]
    eio = lax.broadcasted_iota(I32, logits.shape, 0).astype(F32)
    vals, sels = [], []
    work = logits
    for kk in range(TOP_K):
        m = jnp.max(work, axis=0, keepdims=True)
        idx = jnp.min(jnp.where(work == m, eio, float(N_EXPERTS)), axis=0, keepdims=True)
        sel = eio == idx
        vals.append(m)
        sels.append(sel)
        eidx_ref[kk:kk + 1, :] = idx.astype(I32)
        work = jnp.where(sel, -jnp.inf, work)
    exps = [jnp.exp(vv - vals[0]) for vv in vals]
    tot = functools.reduce(lambda a, c: a + c, exps)
    member = functools.reduce(lambda a, c: a + c, [s_.astype(F32) for s_ in sels])
    before = _dot(member.astype(BF16), upper_ref[...])
    base = base_ref[:, 0:1]
    slot = before + base
    for kk in range(TOP_K):
        gate_ref[kk:kk + 1, :] = exps[kk] / tot
        rank_ref[kk:kk + 1, :] = jnp.sum(jnp.where(sels[kk], slot, 0.0), axis=0, keepdims=True).astype(I32)
    base_new = base + jnp.sum(member, axis=1, keepdims=True)
    base_ref[...] = jnp.broadcast_to(base_new, base_ref.shape)
    cnt_ref[...] = jnp.broadcast_to(base_new, cnt_ref.shape)


def _mixer_call(x, ada_l, n1g, win, convw, cng, lb, hng, wout, gmean, n2g, wrt, br, upper):
    bsz, seq, d = x.shape
    t = MIX_ROWS
    n_tok = bsz * seq
    conv_ch = convw.shape[1]
    hgrn_w = lb.shape[1]
    heads = hgrn_w // HEAD_DIM
    steps = seq // t
    full = lambda a: pl.BlockSpec(a.shape, lambda b, i: (0,) * a.ndim)
    tokmap = lambda b, i: (0, b * steps + i)
    kern = functools.partial(_mixer_kernel, conv_ch=conv_ch, hgrn_w=hgrn_w)
    return pl.pallas_call(
        kern,
        grid=(bsz, steps),
        in_specs=[
            pl.BlockSpec((1, t, d), lambda b, i: (b, i, 0)),
            pl.BlockSpec((1, 6, d), lambda b, i: (b, 0, 0)),
            full(n1g), full(win), full(convw), full(cng), full(lb), full(hng), full(wout),
            full(gmean), full(n2g), full(wrt), full(br), full(upper),
        ],
        out_specs=[
            pl.BlockSpec((1, t, d), lambda b, i: (b, i, 0)),
            pl.BlockSpec((t, d // 2), lambda b, i: (b * steps + i, 0)),
            pl.BlockSpec((TOP_K, t), tokmap),
            pl.BlockSpec((TOP_K, t), tokmap),
            pl.BlockSpec((TOP_K, t), tokmap),
            pl.BlockSpec((N_EXPERTS, 128), lambda b, i: (0, 0)),
        ],
        out_shape=[
            jax.ShapeDtypeStruct((bsz, seq, d), F32),
            jax.ShapeDtypeStruct((n_tok, d // 2), I32),
            jax.ShapeDtypeStruct((TOP_K, n_tok), I32),
            jax.ShapeDtypeStruct((TOP_K, n_tok), F32),
            jax.ShapeDtypeStruct((TOP_K, n_tok), I32),
            jax.ShapeDtypeStruct((N_EXPERTS, 128), F32),
        ],
        scratch_shapes=[
            pltpu.VMEM((heads, HEAD_DIM, HEAD_DIM), F32),
            pltpu.VMEM((SUBLANES, conv_ch), F32),
            pltpu.VMEM((N_EXPERTS, 128), F32),
        ],
        compiler_params=pltpu.CompilerParams(
            dimension_semantics=("arbitrary", "arbitrary"), vmem_limit_bytes=VMEM_LIMIT),
        name="token_mixer",
    )(x, ada_l, n1g, win, convw, cng, lb, hng, wout, gmean, n2g, wrt, br, upper)


def _dispatch_kernel(pos_ref, src_ref, xs_in_ref, xs_ref, sem):
    del xs_in_ref
    t_rows = src_ref.shape[0]

    def row_copy(t, dst_row):
        return pltpu.make_async_copy(src_ref.at[pl.ds(t, 1)], xs_ref.at[pl.ds(dst_row, 1)], sem)

    def start(t, carry):
        for kk in range(TOP_K):
            row_copy(t, pos_ref[kk, t]).start()
        return carry

    def wait(t, carry):
        for kk in range(TOP_K):
            row_copy(t, pos_ref[kk, t]).wait()
        return carry

    lax.fori_loop(0, t_rows, start, 0)
    lax.fori_loop(0, t_rows, wait, 0)


def _dispatch_call(hn2p, pos, xs_init):
    n_tok, dh = hn2p.shape
    t = DISPATCH_ROWS
    return pl.pallas_call(
        _dispatch_kernel,
        grid=(n_tok // t,),
        in_specs=[
            pl.BlockSpec((TOP_K, t), lambda i: (0, i), memory_space=pltpu.SMEM),
            pl.BlockSpec((t, dh), lambda i: (i, 0)),
            pl.BlockSpec(memory_space=pl.ANY),
        ],
        out_specs=pl.BlockSpec(memory_space=pl.ANY),
        out_shape=jax.ShapeDtypeStruct(xs_init.shape, I32),
        scratch_shapes=[pltpu.SemaphoreType.DMA(())],
        input_output_aliases={2: 0},
        compiler_params=pltpu.CompilerParams(
            dimension_semantics=("arbitrary",), vmem_limit_bytes=VMEM_LIMIT, has_side_effects=True),
        name="moe_dispatch",
    )(pos, hn2p, xs_init)


def _expert_kernel(eid_ref, nblk_ref, xs_ref, wgu_ref, bgu_ref, wdn_ref, bdn_ref, ys_ref, wgu_bf, wdn_bf):
    i = pl.program_id(0)
    active = i < nblk_ref[0]
    new_expert = (i == 0) | (eid_ref[i] != eid_ref[jnp.maximum(i - 1, 0)])

    @pl.when(active & new_expert)
    def _():
        def cast_rows(r, carry):
            rows = pl.ds(pl.multiple_of(r * WEIGHT_CAST_ROWS, WEIGHT_CAST_ROWS), WEIGHT_CAST_ROWS)
            wgu_bf[rows, :] = wgu_ref[0, 0, rows, :].astype(BF16)
            wdn_bf[rows, :] = wdn_ref[0, 0, rows, :].astype(BF16)
            return carry
        lax.fori_loop(0, wgu_bf.shape[0] // WEIGHT_CAST_ROWS, cast_rows, 0)

    @pl.when(active)
    def _():
        lo, hi = _unpack_bf16_pair(xs_ref[...])
        dh = lo.shape[1]
        hgu = (_dot(lo.astype(BF16), wgu_bf[0:dh, :]) + _dot(hi.astype(BF16), wgu_bf[dh:2 * dh, :])
               + bgu_ref[0, 0])
        f = hgu.shape[1] // 2
        a = jnp.minimum(hgu[:, 0:f], SWIGLU_LIMIT)
        g = jnp.clip(hgu[:, f:2 * f], -SWIGLU_LIMIT, SWIGLU_LIMIT)
        act = a * _sigmoid(SWIGLU_ALPHA * a) * (g + 1.0)
        y = _dot(act.astype(BF16), wdn_bf[...]) + bdn_ref[0, 0]
        do = y.shape[1] // 2
        ys_ref[...] = _pack_bf16_pair(y[:, 0:do], y[:, do:2 * do])

    @pl.when(jnp.logical_not(active))
    def _():
        ys_ref[...] = jnp.zeros_like(ys_ref)


def _expert_call(layer, block_eid, n_used, xs, wgu, bgu, wdn, bdn):
    p_rows, dh = xs.shape
    bm = EXPERT_ROWS
    depth, n_e, d, f2 = wgu.shape
    dout = wdn.shape[3]
    assert f2 // 2 == d and d % WEIGHT_CAST_ROWS == 0
    grid_spec = pltpu.PrefetchScalarGridSpec(
        num_scalar_prefetch=2,
        grid=(p_rows // bm,),
        in_specs=[
            pl.BlockSpec((bm, dh), lambda i, eid, nb: (i, 0)),
            pl.BlockSpec((1, 1, d, f2), lambda i, eid, nb: (layer, eid[i], 0, 0)),
            pl.BlockSpec((1, 1, 1, f2), lambda i, eid, nb: (layer, eid[i], 0, 0)),
            pl.BlockSpec((1, 1, f2 // 2, dout), lambda i, eid, nb: (layer, eid[i], 0, 0)),
            pl.BlockSpec((1, 1, 1, dout), lambda i, eid, nb: (layer, eid[i], 0, 0)),
        ],
        out_specs=pl.BlockSpec((bm, dout // 2), lambda i, eid, nb: (i, 0)),
        scratch_shapes=[pltpu.VMEM((d, f2), BF16), pltpu.VMEM((f2 // 2, dout), BF16)],
    )
    return pl.pallas_call(
        _expert_kernel,
        grid_spec=grid_spec,
        out_shape=jax.ShapeDtypeStruct((p_rows, dout // 2), I32),
        compiler_params=pltpu.CompilerParams(
            dimension_semantics=("arbitrary",), vmem_limit_bytes=VMEM_LIMIT),
        name="moe_experts",
    )(block_eid, n_used, xs, wgu, bgu.reshape(depth, n_e, 1, f2), wdn, bdn.reshape(depth, n_e, 1, dout))


def _combine_kernel(pos_ref, x_ref, gate_ref, g2_ref, fg_ref, ys_ref, o_ref, buf_ref, sem, *, final_norm):
    t_rows = x_ref.shape[0]

    def row_copy(kk, t, src_row):
        return pltpu.make_async_copy(ys_ref.at[pl.ds(src_row, 1)], buf_ref.at[kk, pl.ds(t, 1)], sem)

    def start(t, carry):
        for kk in range(TOP_K):
            row_copy(kk, t, pos_ref[kk, t]).start()
        return carry

    def wait(t, carry):
        for kk in range(TOP_K):
            row_copy(kk, t, pos_ref[kk, t]).wait()
        return carry

    lax.fori_loop(0, t_rows, start, 0)
    lax.fori_loop(0, t_rows, wait, 0)

    dh = buf_ref.shape[2]
    acc_lo = jnp.zeros((t_rows, dh), F32)
    acc_hi = jnp.zeros((t_rows, dh), F32)
    for kk in range(TOP_K):
        lo, hi = _unpack_bf16_pair(buf_ref[kk])
        gk = gate_ref[:, kk:kk + 1]
        acc_lo = acc_lo + gk * lo
        acc_hi = acc_hi + gk * hi
    x = x_ref[...]
    g2 = g2_ref[0]
    out = x + g2 * jnp.concatenate([acc_lo, acc_hi], axis=1)
    if final_norm:
        out = _rms(out) * fg_ref[...]
    o_ref[...] = out


def _combine_call(x2d, gates_t, pos, g2, final_g, ys, seq, final_norm):
    n_tok, d = x2d.shape
    t = DISPATCH_ROWS
    steps = seq // t
    kern = functools.partial(_combine_kernel, final_norm=final_norm)
    return pl.pallas_call(
        kern,
        grid=(n_tok // t,),
        in_specs=[
            pl.BlockSpec((TOP_K, t), lambda i: (0, i), memory_space=pltpu.SMEM),
            pl.BlockSpec((t, d), lambda i: (i, 0)),
            pl.BlockSpec((t, TOP_K), lambda i: (i, 0)),
            pl.BlockSpec((1, 1, d), lambda i: (i // steps, 0, 0)),
            pl.BlockSpec((1, d), lambda i: (0, 0)),
            pl.BlockSpec(memory_space=pl.ANY),
        ],
        out_specs=pl.BlockSpec((t, d), lambda i: (i, 0)),
        out_shape=jax.ShapeDtypeStruct((n_tok, d), F32),
        scratch_shapes=[pltpu.VMEM((TOP_K, t, d // 2), I32), pltpu.SemaphoreType.DMA(())],
        compiler_params=pltpu.CompilerParams(
            dimension_semantics=("arbitrary",), vmem_limit_bytes=VMEM_LIMIT),
        name="moe_combine",
    )(pos, x2d, gates_t, g2, final_g, ys)


def _route_tables(eidx, rank, cnt, n_blocks):
    bm = EXPERT_ROWS
    counts = cnt[:, 0].astype(I32)
    padded = ((counts + bm - 1) // bm) * bm
    pends = jnp.cumsum(padded)
    pstarts = pends - padded
    pos = rank
    for e in range(N_EXPERTS):
        pos = pos + jnp.where(eidx == e, pstarts[e], 0)
    block_row = jnp.arange(n_blocks, dtype=I32) * bm
    block_eid = jnp.minimum(jnp.sum((pends[None, :] <= block_row[:, None]).astype(I32), axis=1), N_EXPERTS - 1)
    n_used = (pends[-1:] // bm).astype(I32)
    return pos, block_eid, n_used


def _moe_layer(layer, xn, hn2p, eidx, gates, rank, cnt, g2, wgu, bgu, wdn, bdn, final_g, final_norm):
    bsz, seq, d = xn.shape
    n_tok = bsz * seq
    n_blocks = n_tok * TOP_K // EXPERT_ROWS + N_EXPERTS
    pos, block_eid, n_used = _route_tables(eidx, rank, cnt, n_blocks)
    xs = _dispatch_call(hn2p, pos, jnp.zeros((n_blocks * EXPERT_ROWS, d // 2), I32))
    ys = _expert_call(layer, block_eid, n_used, xs, wgu, bgu, wdn, bdn)
    out = _combine_call(xn.reshape(n_tok, d), gates.T, pos, g2, final_g, ys, seq, final_norm)
    return out.reshape(bsz, seq, d)


def kernel(x, c, w_ada, b_ada, norm1_g, norm2_g, w_in, conv_w, conv_norm_g, lower_bounds, hgrn_norm_g, w_out,
           w_router, b_router, w_gu, b_gu, w_down, b_down, final_g):
    depth = w_ada.shape[0]
    bsz, seq, d = x.shape
    conv_ch = conv_w.shape[2]
    c_pad = jnp.zeros((SUBLANES, d), F32).at[:bsz].set(c)
    ada = _ada_call(c_pad, w_ada, b_ada)[:, :bsz].reshape(depth, bsz, 6, d)
    lb_all = _bounds_call(lower_bounds)

    ci = jnp.arange(conv_ch)
    gmean = jnp.where((ci[:, None] // CONV_GROUP) == (ci[None, :] // CONV_GROUP), 1.0 / CONV_GROUP, 0.0).astype(BF16)
    ti = jnp.arange(MIX_ROWS)
    upper = (ti[:, None] < ti[None, :]).astype(BF16)

    for l in range(depth):
        xn, hn2p, eidx, gates, rank, cnt = _mixer_call(
            x, ada[l], norm1_g[l][None], w_in[l].astype(BF16), conv_w[l], conv_norm_g[l][None], lb_all[l][None],
            hgrn_norm_g[l][None], w_out[l].astype(BF16), gmean, norm2_g[l][None], w_router[l].T,
            jnp.broadcast_to(b_router[l][:, None], (N_EXPERTS, 128)), upper)
        x = _moe_layer(l, xn, hn2p, eidx, gates, rank, cnt, ada[l][:, 5:6, :], w_gu, b_gu, w_down, b_down,
                       final_g[None], final_norm=(l == depth - 1))
    return x
```

```python
import functools

import jax
import jax.numpy as jnp
from jax import lax
from jax.experimental import pallas as pl
from jax.experimental.pallas import tpu as pltpu
from jax.experimental.pallas import tpu_sc as plsc

F32 = jnp.float32
BF16 = jnp.bfloat16
I32 = jnp.int32
HIGHEST = lax.Precision.HIGHEST

RMS_EPS = 1e-6
N_EXPERTS = 32
TOP_K = 4
CONV_GROUP = 64
HEAD_DIM = 128
SWIGLU_LIMIT = 7.0
SWIGLU_ALPHA = 1.702
SUBLANES = 8
VMEM_LIMIT = 56 * 1024 * 1024

MIX_ROWS = 256
EXPERT_ROWS = 256
COMBINE_ROWS = 512
SC_CORES = 2
SC_SUBCORES = 16
SC_WORKERS = SC_CORES * SC_SUBCORES
SC_CHUNK = 64
ADA_COLS = 1536
WEIGHT_CAST_ROWS = 128


def _dot(a, b):
    return jnp.dot(a, b, preferred_element_type=F32)


def _dot_nt(a, b, precision=None):
    return lax.dot_general(a, b, (((1,), (1,)), ((), ())), precision=precision,
                           preferred_element_type=F32)


def _dot_tn(a, b):
    return lax.dot_general(a, b, (((0,), (0,)), ((), ())), preferred_element_type=F32)


def _sigmoid(x):
    return 1.0 / (1.0 + jnp.exp(-x))


def _rms(x):
    return x * lax.rsqrt(jnp.mean(x * x, axis=-1, keepdims=True) + RMS_EPS)


def _pack_bf16_pair(lo, hi):
    lo_bits = lax.bitcast_convert_type(lo.astype(BF16).astype(F32), I32)
    hi_bits = lax.bitcast_convert_type(hi.astype(BF16).astype(F32), I32)
    return lax.shift_right_logical(lo_bits, 16) | (hi_bits & jnp.int32(-65536))


def _unpack_bf16_pair(p):
    lo = lax.bitcast_convert_type(lax.shift_left(p, 16), F32)
    hi = lax.bitcast_convert_type(p & jnp.int32(-65536), F32)
    return lo, hi


def _ada_kernel(c_ref, w_ref, b_ref, o_ref):
    cv = c_ref[...]
    ca = cv * _sigmoid(cv)
    o_ref[0] = jnp.dot(ca, w_ref[0], precision=HIGHEST, preferred_element_type=F32) + b_ref[0]


def _ada_call(c_pad, w_ada, b_ada):
    depth, d, n6 = w_ada.shape
    rows = c_pad.shape[0]
    return pl.pallas_call(
        _ada_kernel,
        grid=(depth, n6 // ADA_COLS),
        in_specs=[
            pl.BlockSpec((rows, d), lambda l, j: (0, 0)),
            pl.BlockSpec((1, d, ADA_COLS), lambda l, j: (l, 0, j)),
            pl.BlockSpec((1, 1, ADA_COLS), lambda l, j: (l, 0, j)),
        ],
        out_specs=pl.BlockSpec((1, rows, ADA_COLS), lambda l, j: (l, 0, j)),
        out_shape=jax.ShapeDtypeStruct((depth, rows, n6), F32),
        compiler_params=pltpu.CompilerParams(
            dimension_semantics=("arbitrary", "arbitrary"), vmem_limit_bytes=VMEM_LIMIT),
        name="ada_proj",
    )(c_pad, w_ada, b_ada.reshape(depth, 1, n6))


def _bounds_kernel(lb_ref, o_ref):
    rows = [lb_ref[l:l + 1, :] for l in range(lb_ref.shape[0])]
    m = functools.reduce(jnp.maximum, rows)
    es = [jnp.exp(r - m) for r in rows]
    tot = functools.reduce(lambda a, b: a + b, es)
    cum = None
    first = None
    for l, e in enumerate(es):
        p = e / tot
        cum = p if cum is None else cum + p
        if first is None:
            first = cum
        o_ref[l:l + 1, :] = cum - first


def _bounds_call(lower_bounds):
    return pl.pallas_call(
        _bounds_kernel,
        out_shape=jax.ShapeDtypeStruct(lower_bounds.shape, F32),
        name="hgrn_bounds",
    )(lower_bounds.astype(F32))


def _hgrn_head(q, k, v, g_cum, st_ref, h, row):
    t_rows = q.shape[0]
    col = lax.broadcasted_iota(I32, (t_rows, t_rows), 1)
    rowsq = lax.broadcasted_iota(I32, (t_rows, t_rows), 0)
    scores = jnp.zeros((t_rows, t_rows), F32)
    half = t_rows // 2
    while half >= SUBLANES:
        grp = 2 * half
        g_mid = g_cum.reshape(t_rows // grp, grp, HEAD_DIM)[:, half - 1:half, :]
        g_mid = jnp.broadcast_to(g_mid, (t_rows // grp, grp, HEAD_DIM)).reshape(t_rows, HEAD_DIM)
        upper = (row & half) != 0
        d = g_cum - g_mid
        e = jnp.exp(jnp.where(upper, d, -d))
        qh = jnp.where(upper, q * e, 0.0).astype(BF16)
        kh = jnp.where(upper, 0.0, k * e).astype(BF16)
        sc = _dot_nt(qh, kh)
        if grp < t_rows:
            shift = grp.bit_length() - 1
            sc = jnp.where(lax.shift_right_logical(rowsq, shift) == lax.shift_right_logical(col, shift),
                           sc, 0.0)
        scores = scores + sc
        half //= 2

    out = _dot(scores.astype(BF16), v.astype(BF16))
    sub = row & (SUBLANES - 1)
    for delta in range(SUBLANES):
        if delta == 0:
            w = q * k
            vr = v
        else:
            kr = pltpu.roll(k, delta, 0)
            gr = pltpu.roll(g_cum, delta, 0)
            vr = pltpu.roll(v, delta, 0)
            valid = sub >= delta
            w = jnp.where(valid, q * kr * jnp.exp(jnp.where(valid, g_cum - gr, 0.0)), 0.0)
        out = out + jnp.sum(w, axis=-1, keepdims=True) * vr

    st = st_ref[h]
    out = out + _dot_nt((q * jnp.exp(g_cum)).astype(BF16), st.astype(BF16))
    g_last = g_cum[t_rows - 1:t_rows, :]
    kd = (k * jnp.exp(g_last - g_cum)).astype(BF16)
    st_ref[h] = st * jnp.exp(g_last) + _dot_tn(v.astype(BF16), kd)
    return out


def _mixer_kernel(x_ref, ada_ref, n1g_ref, win_ref, convw_ref, cng_ref, lb_ref, hng_ref, wout_ref,
                  gmean_ref, n2g_ref, wrt_ref, br_ref, upper_ref,
                  xo_ref, hn2_ref, eidx_ref, gate_ref, rank_ref, cnt_ref,
                  st_ref, carry_ref, base_ref, *, conv_ch, hgrn_w):
    b = pl.program_id(0)
    i = pl.program_id(1)
    t_rows = x_ref.shape[1]
    heads = hgrn_w // HEAD_DIM

    @pl.when(i == 0)
    def _():
        st_ref[...] = jnp.zeros_like(st_ref)
        carry_ref[...] = jnp.zeros_like(carry_ref)

    @pl.when((b == 0) & (i == 0))
    def _():
        base_ref[...] = jnp.zeros_like(base_ref)

    sh1, sc1, g1 = ada_ref[0, 0:1, :], ada_ref[0, 1:2, :], ada_ref[0, 2:3, :]
    sh2, sc2, g2 = ada_ref[0, 3:4, :], ada_ref[0, 4:5, :], ada_ref[0, 5:6, :]
    del g2

    x = x_ref[0]
    hn = (_rms(x) * n1g_ref[...]) * (1.0 + sc1) + sh1
    proj = _dot(hn.astype(BF16), win_ref[...])
    c0 = conv_ch
    cb, cc, ch = proj[:, 0:c0], proj[:, c0:2 * c0], proj[:, 2 * c0:3 * c0]
    o0 = 3 * c0
    q_raw = proj[:, o0:o0 + hgrn_w]
    f_raw = proj[:, o0 + hgrn_w:o0 + 2 * hgrn_w]
    v_all = proj[:, o0 + 2 * hgrn_w:o0 + 3 * hgrn_w]
    og = proj[:, o0 + 3 * hgrn_w:o0 + 4 * hgrn_w]

    u = cc * ch
    carry = carry_ref[...]
    row8 = lax.broadcasted_iota(I32, (SUBLANES, c0), 0)
    u1 = pltpu.roll(u, 1, 0)
    u2 = pltpu.roll(u, 2, 0)
    u1 = jnp.concatenate([jnp.where(row8 < 1, pltpu.roll(carry, 1, 0), u1[0:SUBLANES]), u1[SUBLANES:]], axis=0)
    u2 = jnp.concatenate([jnp.where(row8 < 2, pltpu.roll(carry, 2, 0), u2[0:SUBLANES]), u2[SUBLANES:]], axis=0)
    carry_ref[...] = u[t_rows - SUBLANES:t_rows]
    yc = cb * (convw_ref[0:1, :] * u2 + convw_ref[1:2, :] * u1 + convw_ref[2:3, :] * u)
    sq = yc * yc
    sq_hi = sq.astype(BF16)
    sq_lo = (sq - sq_hi.astype(F32)).astype(BF16)
    gms = _dot(sq_hi, gmean_ref[...]) + _dot(sq_lo, gmean_ref[...])
    ycn = yc * lax.rsqrt(gms + RMS_EPS) * cng_ref[...]

    lb = lb_ref[...]
    log_lb = jnp.log(lb)
    log_1mlb = jnp.log1p(-lb)
    log_sig = jnp.minimum(f_raw, 0.0) - jnp.log1p(jnp.exp(-jnp.abs(f_raw)))
    bb = log_1mlb + log_sig
    log_f = jnp.maximum(log_lb, bb) + jnp.log1p(jnp.exp(-jnp.abs(log_lb - bb)))
    k_all = 1.0 - jnp.exp(log_f)
    q_all = q_raw * _sigmoid(q_raw)
    roww = lax.broadcasted_iota(I32, (t_rows, hgrn_w), 0)
    g_cum = log_f
    s = 1
    while s < t_rows:
        g_cum = g_cum + jnp.where(roww >= s, pltpu.roll(g_cum, s, 0), 0.0)
        s *= 2
    row = lax.broadcasted_iota(I32, (t_rows, HEAD_DIM), 0)
    outs = []
    for h in range(heads):
        sl = slice(h * HEAD_DIM, (h + 1) * HEAD_DIM)
        o = _hgrn_head(q_all[:, sl], k_all[:, sl], v_all[:, sl], g_cum[:, sl], st_ref, h, row)
        outs.append(_rms(o))
    oh = jnp.concatenate(outs, axis=1) * hng_ref[...] * (og * _sigmoid(og))

    mix = _dot(ycn.astype(BF16), wout_ref[0:c0, :]) + _dot(oh.astype(BF16), wout_ref[c0:c0 + hgrn_w, :])
    xn = x + g1 * mix
    xo_ref[0] = xn

    hn2 = (_rms(xn) * n2g_ref[...]) * (1.0 + sc2) + sh2
    dh = hn2.shape[1] // 2
    hn2_ref[...] = _pack_bf16_pair(hn2[:, 0:dh], hn2[:, dh:2 * dh])
    logits = _dot_nt(wrt_ref[...], hn2, precision=HIGHEST) + br_ref[:, 0:1]
    eio = lax.broadcasted_iota(I32, logits.shape, 0).astype(F32)
    vals, sels = [], []
    work = logits
    for kk in range(TOP_K):
        m = jnp.max(work, axis=0, keepdims=True)
        idx = jnp.min(jnp.where(work == m, eio, float(N_EXPERTS)), axis=0, keepdims=True)
        sel = eio == idx
        vals.append(m)
        sels.append(sel)
        eidx_ref[kk:kk + 1, :] = idx.astype(I32)
        work = jnp.where(sel, -jnp.inf, work)
    exps = [jnp.exp(vv - vals[0]) for vv in vals]
    tot = functools.reduce(lambda a, c: a + c, exps)
    member = functools.reduce(lambda a, c: a + c, [s_.astype(F32) for s_ in sels])
    before = _dot(member.astype(BF16), upper_ref[...])
    base = base_ref[:, 0:1]
    slot = before + base
    for kk in range(TOP_K):
        gate_ref[kk:kk + 1, :] = exps[kk] / tot
        rank_ref[kk:kk + 1, :] = jnp.sum(jnp.where(sels[kk], slot, 0.0), axis=0, keepdims=True).astype(I32)
    base_new = base + jnp.sum(member, axis=1, keepdims=True)
    base_ref[...] = jnp.broadcast_to(base_new, base_ref.shape)
    cnt_ref[...] = jnp.broadcast_to(base_new, cnt_ref.shape)


def _mixer_call(x, ada_l, n1g, win, convw, cng, lb, hng, wout, gmean, n2g, wrt, br, upper):
    bsz, seq, d = x.shape
    t = MIX_ROWS
    n_tok = bsz * seq
    conv_ch = convw.shape[1]
    hgrn_w = lb.shape[1]
    heads = hgrn_w // HEAD_DIM
    steps = seq // t
    full = lambda a: pl.BlockSpec(a.shape, lambda b, i: (0,) * a.ndim)
    tokmap = lambda b, i: (0, b * steps + i)
    kern = functools.partial(_mixer_kernel, conv_ch=conv_ch, hgrn_w=hgrn_w)
    return pl.pallas_call(
        kern,
        grid=(bsz, steps),
        in_specs=[
            pl.BlockSpec((1, t, d), lambda b, i: (b, i, 0)),
            pl.BlockSpec((1, 6, d), lambda b, i: (b, 0, 0)),
            full(n1g), full(win), full(convw), full(cng), full(lb), full(hng), full(wout),
            full(gmean), full(n2g), full(wrt), full(br), full(upper),
        ],
        out_specs=[
            pl.BlockSpec((1, t, d), lambda b, i: (b, i, 0)),
            pl.BlockSpec((t, d // 2), lambda b, i: (b * steps + i, 0)),
            pl.BlockSpec((TOP_K, t), tokmap),
            pl.BlockSpec((TOP_K, t), tokmap),
            pl.BlockSpec((TOP_K, t), tokmap),
            pl.BlockSpec((N_EXPERTS, 128), lambda b, i: (0, 0)),
        ],
        out_shape=[
            jax.ShapeDtypeStruct((bsz, seq, d), F32),
            jax.ShapeDtypeStruct((n_tok, d // 2), I32),
            jax.ShapeDtypeStruct((TOP_K, n_tok), I32),
            jax.ShapeDtypeStruct((TOP_K, n_tok), F32),
            jax.ShapeDtypeStruct((TOP_K, n_tok), I32),
            jax.ShapeDtypeStruct((N_EXPERTS, 128), F32),
        ],
        scratch_shapes=[
            pltpu.VMEM((heads, HEAD_DIM, HEAD_DIM), F32),
            pltpu.VMEM((SUBLANES, conv_ch), F32),
            pltpu.VMEM((N_EXPERTS, 128), F32),
        ],
        compiler_params=pltpu.CompilerParams(
            dimension_semantics=("arbitrary", "arbitrary"), vmem_limit_bytes=VMEM_LIMIT),
        name="token_mixer",
    )(x, ada_l, n1g, win, convw, cng, lb, hng, wout, gmean, n2g, wrt, br, upper)


def _sc_worker():
    return lax.axis_index("s") * SC_CORES + lax.axis_index("c")


def _sc_mesh():
    return plsc.VectorSubcoreMesh(core_axis_name="c", subcore_axis_name="s")


def _sc_scatter_rows(src, pos3d, p_rows):
    kk, n_idx_rows, ch = pos3d.shape
    d = src.shape[1]
    n_ch = n_idx_rows // SC_WORKERS
    assert n_idx_rows % SC_WORKERS == 0 and n_ch % 2 == 0

    @functools.partial(
        pl.kernel, mesh=_sc_mesh(), out_type=jax.ShapeDtypeStruct((p_rows, d), src.dtype),
        scratch_types=[pltpu.VMEM((kk, n_ch, ch), I32), pltpu.VMEM((2, ch, d), src.dtype),
                       pltpu.SemaphoreType.DMA((2,)), pltpu.SemaphoreType.DMA((2,))],
        name="sc_scatter_rows")
    def scatter_kernel(src_hbm, pos_hbm, out_hbm, idx_v, rows_v, lsem, ssem):
        wid = _sc_worker()
        for j in range(kk):
            pltpu.sync_copy(pos_hbm.at[j, pl.ds(wid * n_ch, n_ch)], idx_v.at[j])
        base = wid * (n_ch * ch)

        def load(c, slot):
            return pltpu.make_async_copy(src_hbm.at[pl.ds(base + c * ch, ch)], rows_v.at[slot], lsem.at[slot])

        def scatter(c, slot, j):
            return pltpu.make_async_copy(rows_v.at[slot], out_hbm.at[idx_v.at[j, c]], ssem.at[slot])

        load(0, 0).start()

        @pl.loop(0, n_ch, step=2)
        def _(c0):
            for slot in range(2):
                c = c0 + slot
                other = 1 - slot

                @pl.when(c >= 1)
                def _():
                    for j in range(kk):
                        scatter(c - 1, other, j).wait()

                @pl.when(c + 1 < n_ch)
                def _():
                    load(c + 1, other).start()

                load(c, slot).wait()
                for j in range(kk):
                    scatter(c, slot, j).start()

        for j in range(kk):
            scatter(n_ch - 1, (n_ch - 1) % 2, j).wait()

    return scatter_kernel(src, pos3d)


def _sc_gather_rows(table, idx2d):
    n_idx_rows, ch = idx2d.shape
    d = table.shape[1]
    n_ch = n_idx_rows // SC_WORKERS
    assert n_idx_rows % SC_WORKERS == 0 and n_ch % 2 == 0

    @functools.partial(
        pl.kernel, mesh=_sc_mesh(), out_type=jax.ShapeDtypeStruct((n_idx_rows * ch, d), table.dtype),
        scratch_types=[pltpu.VMEM((n_ch, ch), I32), pltpu.VMEM((2, ch, d), table.dtype),
                       pltpu.SemaphoreType.DMA((2,)), pltpu.SemaphoreType.DMA((2,))],
        name="sc_gather_rows")
    def gather_kernel(table_hbm, idx_hbm, out_hbm, idx_v, rows_v, gsem, wsem):
        wid = _sc_worker()
        pltpu.sync_copy(idx_hbm.at[pl.ds(wid * n_ch, n_ch)], idx_v)
        base = wid * (n_ch * ch)

        def gather(c, slot):
            return pltpu.make_async_copy(table_hbm.at[idx_v.at[c]], rows_v.at[slot], gsem.at[slot])

        def write(c, slot):
            return pltpu.make_async_copy(rows_v.at[slot], out_hbm.at[pl.ds(base + c * ch, ch)], wsem.at[slot])

        gather(0, 0).start()

        @pl.loop(0, n_ch, step=2)
        def _(c0):
            for slot in range(2):
                c = c0 + slot
                other = 1 - slot

                @pl.when(c >= 1)
                def _():
                    write(c - 1, other).wait()

                @pl.when(c + 1 < n_ch)
                def _():
                    gather(c + 1, other).start()

                gather(c, slot).wait()
                write(c, slot).start()

        write(n_ch - 1, (n_ch - 1) % 2).wait()

    return gather_kernel(table, idx2d)


def _expert_kernel(eid_ref, nvalid_ref, xs_ref, wgu_ref, bgu_ref, wdn_ref, bdn_ref, ys_ref, wgu_bf, wdn_bf):
    i = pl.program_id(0)
    nvalid = nvalid_ref[i]
    active = nvalid > 0
    new_expert = (i == 0) | (eid_ref[i] != eid_ref[jnp.maximum(i - 1, 0)])

    @pl.when(active & new_expert)
    def _():
        def cast_rows(r, carry):
            rows = pl.ds(pl.multiple_of(r * WEIGHT_CAST_ROWS, WEIGHT_CAST_ROWS), WEIGHT_CAST_ROWS)
            wgu_bf[rows, :] = wgu_ref[0, 0, rows, :].astype(BF16)
            wdn_bf[rows, :] = wdn_ref[0, 0, rows, :].astype(BF16)
            return carry
        lax.fori_loop(0, wgu_bf.shape[0] // WEIGHT_CAST_ROWS, cast_rows, 0)

    @pl.when(active)
    def _():
        packed = xs_ref[...]
        keep = lax.broadcasted_iota(I32, packed.shape, 0) < nvalid
        lo, hi = _unpack_bf16_pair(jnp.where(keep, packed, 0))
        dh = lo.shape[1]
        hgu = (_dot(lo.astype(BF16), wgu_bf[0:dh, :]) + _dot(hi.astype(BF16), wgu_bf[dh:2 * dh, :])
               + bgu_ref[0, 0])
        f = hgu.shape[1] // 2
        a = jnp.minimum(hgu[:, 0:f], SWIGLU_LIMIT)
        g = jnp.clip(hgu[:, f:2 * f], -SWIGLU_LIMIT, SWIGLU_LIMIT)
        act = a * _sigmoid(SWIGLU_ALPHA * a) * (g + 1.0)
        y = _dot(act.astype(BF16), wdn_bf[...]) + bdn_ref[0, 0]
        do = y.shape[1] // 2
        ys_ref[...] = _pack_bf16_pair(y[:, 0:do], y[:, do:2 * do])

    @pl.when(jnp.logical_not(active))
    def _():
        ys_ref[...] = jnp.zeros_like(ys_ref)


def _expert_call(layer, block_eid, block_valid, xs, wgu, bgu, wdn, bdn):
    p_rows, dh = xs.shape
    bm = EXPERT_ROWS
    depth, n_e, d, f2 = wgu.shape
    dout = wdn.shape[3]
    assert f2 // 2 == d and d % WEIGHT_CAST_ROWS == 0
    grid_spec = pltpu.PrefetchScalarGridSpec(
        num_scalar_prefetch=2,
        grid=(p_rows // bm,),
        in_specs=[
            pl.BlockSpec((bm, dh), lambda i, eid, nv: (i, 0)),
            pl.BlockSpec((1, 1, d, f2), lambda i, eid, nv: (layer, eid[i], 0, 0)),
            pl.BlockSpec((1, 1, 1, f2), lambda i, eid, nv: (layer, eid[i], 0, 0)),
            pl.BlockSpec((1, 1, f2 // 2, dout), lambda i, eid, nv: (layer, eid[i], 0, 0)),
            pl.BlockSpec((1, 1, 1, dout), lambda i, eid, nv: (layer, eid[i], 0, 0)),
        ],
        out_specs=pl.BlockSpec((bm, dout // 2), lambda i, eid, nv: (i, 0)),
        scratch_shapes=[pltpu.VMEM((d, f2), BF16), pltpu.VMEM((f2 // 2, dout), BF16)],
    )
    return pl.pallas_call(
        _expert_kernel,
        grid_spec=grid_spec,
        out_shape=jax.ShapeDtypeStruct((p_rows, dout // 2), I32),
        compiler_params=pltpu.CompilerParams(
            dimension_semantics=("arbitrary",), vmem_limit_bytes=VMEM_LIMIT),
        name="moe_experts",
    )(block_eid, block_valid, xs, wgu, bgu.reshape(depth, n_e, 1, f2), wdn, bdn.reshape(depth, n_e, 1, dout))


def _combine_kernel(x_ref, gate_ref, g2_ref, fg_ref, yg_ref, o_ref, *, final_norm):
    t_rows, d = x_ref.shape
    dh = d // 2
    acc_lo = jnp.zeros((t_rows, dh), F32)
    acc_hi = jnp.zeros((t_rows, dh), F32)
    for kk in range(TOP_K):
        lo, hi = _unpack_bf16_pair(yg_ref[kk])
        gk = gate_ref[:, kk:kk + 1]
        acc_lo = acc_lo + gk * lo
        acc_hi = acc_hi + gk * hi
    out = x_ref[...] + g2_ref[...] * jnp.concatenate([acc_lo, acc_hi], axis=1)
    if final_norm:
        out = _rms(out) * fg_ref[...]
    o_ref[...] = out


def _combine_call(x2d, gates_t, g2, final_g, yg, final_norm):
    n_tok, d = x2d.shape
    t = COMBINE_ROWS
    kern = functools.partial(_combine_kernel, final_norm=final_norm)
    return pl.pallas_call(
        kern,
        grid=(n_tok // t,),
        in_specs=[
            pl.BlockSpec((t, d), lambda i: (i, 0)),
            pl.BlockSpec((t, TOP_K), lambda i: (i, 0)),
            pl.BlockSpec((1, d), lambda i: (0, 0)),
            pl.BlockSpec((1, d), lambda i: (0, 0)),
            pl.BlockSpec((TOP_K, t, d // 2), lambda i: (0, i, 0)),
        ],
        out_specs=pl.BlockSpec((t, d), lambda i: (i, 0)),
        out_shape=jax.ShapeDtypeStruct((n_tok, d), F32),
        compiler_params=pltpu.CompilerParams(
            dimension_semantics=("arbitrary",), vmem_limit_bytes=VMEM_LIMIT),
        name="moe_combine",
    )(x2d, gates_t, g2, final_g, yg)


def _route_tables(eidx, rank, cnt, n_blocks):
    bm = EXPERT_ROWS
    counts = cnt[:, 0].astype(I32)
    padded = ((counts + bm - 1) // bm) * bm
    pends = jnp.cumsum(padded)
    pstarts = pends - padded
    pos = rank
    for e in range(N_EXPERTS):
        pos = pos + jnp.where(eidx == e, pstarts[e], 0)
    block_row = jnp.arange(n_blocks, dtype=I32) * bm
    block_eid = jnp.minimum(jnp.sum((pends[None, :] <= block_row[:, None]).astype(I32), axis=1), N_EXPERTS - 1)
    onehot = block_eid[:, None] == jnp.arange(N_EXPERTS, dtype=I32)[None, :]
    row_end = jnp.sum(jnp.where(onehot, (pstarts + counts)[None, :], 0), axis=1)
    block_valid = jnp.clip(row_end - block_row, 0, bm).astype(I32)
    return pos, block_eid, block_valid


def _moe_rows(layer, hn2p, eidx, rank, cnt, wgu, bgu, wdn, bdn):
    n_tok, dh = hn2p.shape
    n_blocks = n_tok * TOP_K // EXPERT_ROWS + N_EXPERTS
    pos, block_eid, block_valid = _route_tables(eidx, rank, cnt, n_blocks)
    xs = _sc_scatter_rows(hn2p, pos.reshape(TOP_K, n_tok // SC_CHUNK, SC_CHUNK), n_blocks * EXPERT_ROWS)
    ys = _expert_call(layer, block_eid, block_valid, xs, wgu, bgu, wdn, bdn)
    yg = _sc_gather_rows(ys, pos.reshape(TOP_K * n_tok // SC_CHUNK, SC_CHUNK))
    return yg.reshape(TOP_K, n_tok, dh)


def kernel(x, c, w_ada, b_ada, norm1_g, norm2_g, w_in, conv_w, conv_norm_g, lower_bounds, hgrn_norm_g, w_out,
           w_router, b_router, w_gu, b_gu, w_down, b_down, final_g):
    depth = w_ada.shape[0]
    bsz, seq, d = x.shape
    conv_ch = conv_w.shape[2]
    c_pad = jnp.zeros((SUBLANES, d), F32).at[:bsz].set(c)
    ada = _ada_call(c_pad, w_ada, b_ada)[:, :bsz].reshape(depth, bsz, 6, d)
    lb_all = _bounds_call(lower_bounds)

    ci = jnp.arange(conv_ch)
    gmean = jnp.where((ci[:, None] // CONV_GROUP) == (ci[None, :] // CONV_GROUP), 1.0 / CONV_GROUP, 0.0).astype(BF16)
    ti = jnp.arange(MIX_ROWS)
    upper = (ti[:, None] < ti[None, :]).astype(BF16)

    xb = [x[b:b + 1] for b in range(bsz)]
    for l in range(depth):
        win, wout = w_in[l].astype(BF16), w_out[l].astype(BF16)
        wrt = w_router[l].T
        br = jnp.broadcast_to(b_router[l][:, None], (N_EXPERTS, 128))
        mixed = [
            _mixer_call(xb[b], ada[l, b:b + 1], norm1_g[l][None], win, conv_w[l], conv_norm_g[l][None],
                        lb_all[l][None], hgrn_norm_g[l][None], wout, gmean, norm2_g[l][None], wrt, br, upper)
            for b in range(bsz)]
        for b in range(bsz):
            xn, hn2p, eidx, gates, rank, cnt = mixed[b]
            yg = _moe_rows(l, hn2p, eidx, rank, cnt, w_gu, b_gu, w_down, b_down)
            xb[b] = _combine_call(xn.reshape(seq, d), gates.T, ada[l, b, 5:6, :], final_g[None], yg,
                                  final_norm=(l == depth - 1)).reshape(1, seq, d)
    return jnp.concatenate(xb, axis=0)
```

```python
import functools

import jax
import jax.numpy as jnp
from jax import lax
from jax.experimental import pallas as pl
from jax.experimental.pallas import tpu as pltpu
from jax.experimental.pallas import tpu_sc as plsc

F32 = jnp.float32
BF16 = jnp.bfloat16
I32 = jnp.int32
HIGHEST = lax.Precision.HIGHEST

RMS_EPS = 1e-6
N_EXPERTS = 32
TOP_K = 4
CONV_GROUP = 64
HEAD_DIM = 128
SWIGLU_LIMIT = 7.0
SWIGLU_ALPHA = 1.702
SUBLANES = 8
VMEM_LIMIT = 56 * 1024 * 1024

MIX_ROWS = 256
EXPERT_ROWS = 256
COMBINE_ROWS = 512
SC_CORES = 2
SC_SUBCORES = 16
SC_WORKERS = SC_CORES * SC_SUBCORES
SC_CHUNK = 64
ADA_COLS = 1536
WEIGHT_CAST_ROWS = 128


def _dot(a, b):
    return jnp.dot(a, b, preferred_element_type=F32)


def _dot_nt(a, b, precision=None):
    return lax.dot_general(a, b, (((1,), (1,)), ((), ())), precision=precision,
                           preferred_element_type=F32)


def _dot_tn(a, b):
    return lax.dot_general(a, b, (((0,), (0,)), ((), ())), preferred_element_type=F32)


def _sigmoid(x):
    return 1.0 / (1.0 + jnp.exp(-x))


def _rms(x):
    return x * lax.rsqrt(jnp.mean(x * x, axis=-1, keepdims=True) + RMS_EPS)


def _pack_bf16_pair(lo, hi):
    lo_bits = lax.bitcast_convert_type(lo.astype(BF16).astype(F32), I32)
    hi_bits = lax.bitcast_convert_type(hi.astype(BF16).astype(F32), I32)
    return lax.shift_right_logical(lo_bits, 16) | (hi_bits & jnp.int32(-65536))


def _unpack_bf16_pair(p):
    lo = lax.bitcast_convert_type(lax.shift_left(p, 16), F32)
    hi = lax.bitcast_convert_type(p & jnp.int32(-65536), F32)
    return lo, hi


def _ada_kernel(c_ref, w_ref, b_ref, o_ref):
    cv = c_ref[...]
    ca = cv * _sigmoid(cv)
    o_ref[0] = jnp.dot(ca, w_ref[0], precision=HIGHEST, preferred_element_type=F32) + b_ref[0]


def _ada_call(c_pad, w_ada, b_ada):
    depth, d, n6 = w_ada.shape
    rows = c_pad.shape[0]
    return pl.pallas_call(
        _ada_kernel,
        grid=(depth, n6 // ADA_COLS),
        in_specs=[
            pl.BlockSpec((rows, d), lambda l, j: (0, 0)),
            pl.BlockSpec((1, d, ADA_COLS), lambda l, j: (l, 0, j)),
            pl.BlockSpec((1, 1, ADA_COLS), lambda l, j: (l, 0, j)),
        ],
        out_specs=pl.BlockSpec((1, rows, ADA_COLS), lambda l, j: (l, 0, j)),
        out_shape=jax.ShapeDtypeStruct((depth, rows, n6), F32),
        compiler_params=pltpu.CompilerParams(
            dimension_semantics=("arbitrary", "arbitrary"), vmem_limit_bytes=VMEM_LIMIT),
        name="ada_proj",
    )(c_pad, w_ada, b_ada.reshape(depth, 1, n6))


def _bounds_kernel(lb_ref, o_ref):
    rows = [lb_ref[l:l + 1, :] for l in range(lb_ref.shape[0])]
    m = functools.reduce(jnp.maximum, rows)
    es = [jnp.exp(r - m) for r in rows]
    tot = functools.reduce(lambda a, b: a + b, es)
    cum = None
    first = None
    for l, e in enumerate(es):
        p = e / tot
        cum = p if cum is None else cum + p
        if first is None:
            first = cum
        o_ref[l:l + 1, :] = cum - first


def _bounds_call(lower_bounds):
    return pl.pallas_call(
        _bounds_kernel,
        out_shape=jax.ShapeDtypeStruct(lower_bounds.shape, F32),
        name="hgrn_bounds",
    )(lower_bounds.astype(F32))


def _small_level_exponents(g, row):
    t_rows, width = g.shape
    blocks = t_rows // SUBLANES

    def blockroll(x, k):
        return pltpu.roll(x.reshape(blocks, SUBLANES, width), k, 1).reshape(t_rows, width)

    a1 = g + blockroll(g, 1)
    a2 = a1 + blockroll(g, 2)
    a3 = a2 + blockroll(g, 3)
    b1 = blockroll(g, SUBLANES - 1)
    b2 = b1 + blockroll(g, SUBLANES - 2)
    b3 = b2 + blockroll(g, SUBLANES - 3)
    o = row & 7
    o2 = row & 3
    arg1 = jnp.where((row & 1) != 0, g, 0.0)
    arg2 = jnp.where(o2 == 3, a1, jnp.where(o2 == 2, g, jnp.where(o2 == 0, b1, 0.0)))
    up4 = jnp.where(o == 7, a3, jnp.where(o == 6, a2, jnp.where(o == 5, a1, g)))
    lo4 = jnp.where(o == 0, b3, jnp.where(o == 1, b2, jnp.where(o == 2, b1, 0.0)))
    arg4 = jnp.where(o >= 4, up4, lo4)
    return {1: arg1, 2: arg2, 4: arg4}


def _hgrn_head(q, k, v, g_cum, small_args, group_masks, st_ref, h, row):
    t_rows = q.shape[0]
    scores = jnp.zeros((t_rows, t_rows), F32)
    half = t_rows // 2
    while half >= 1:
        grp = 2 * half
        upper = (row & half) != 0
        if half >= SUBLANES:
            g_mid = g_cum.reshape(t_rows // grp, grp, HEAD_DIM)[:, half - 1:half, :]
            g_mid = jnp.broadcast_to(g_mid, (t_rows // grp, grp, HEAD_DIM)).reshape(t_rows, HEAD_DIM)
            d = g_cum - g_mid
            arg = jnp.where(upper, d, -d)
        else:
            arg = small_args[half]
        e = jnp.exp(arg)
        qh = jnp.where(upper, q * e, 0.0).astype(BF16)
        kh = jnp.where(upper, 0.0, k * e).astype(BF16)
        sc = _dot_nt(qh, kh)
        if grp < t_rows:
            sc = jnp.where(group_masks[grp], sc, 0.0)
        scores = scores + sc
        half //= 2

    out = _dot(scores.astype(BF16), v.astype(BF16)) + jnp.sum(q * k, axis=-1, keepdims=True) * v

    st = st_ref[h]
    out = out + _dot_nt((q * jnp.exp(g_cum)).astype(BF16), st.astype(BF16))
    g_last = g_cum[t_rows - 1:t_rows, :]
    kd = (k * jnp.exp(g_last - g_cum)).astype(BF16)
    st_ref[h] = st * jnp.exp(g_last) + _dot_tn(v.astype(BF16), kd)
    return out


def _mixer_kernel(x_ref, ada_ref, n1g_ref, win_ref, convw_ref, cng_ref, lb_ref, hng_ref, wout_ref,
                  gmean_ref, n2g_ref, wrt_ref, br_ref, upper_ref,
                  xo_ref, hn2_ref, eidx_ref, gate_ref, rank_ref, cnt_ref,
                  st_ref, carry_ref, base_ref, *, conv_ch, hgrn_w):
    b = pl.program_id(0)
    i = pl.program_id(1)
    t_rows = x_ref.shape[1]
    heads = hgrn_w // HEAD_DIM

    @pl.when(i == 0)
    def _():
        st_ref[...] = jnp.zeros_like(st_ref)
        carry_ref[...] = jnp.zeros_like(carry_ref)

    @pl.when((b == 0) & (i == 0))
    def _():
        base_ref[...] = jnp.zeros_like(base_ref)

    sh1, sc1, g1 = ada_ref[0, 0:1, :], ada_ref[0, 1:2, :], ada_ref[0, 2:3, :]
    sh2, sc2, g2 = ada_ref[0, 3:4, :], ada_ref[0, 4:5, :], ada_ref[0, 5:6, :]
    del g2

    x = x_ref[0]
    hn = (_rms(x) * n1g_ref[...]) * (1.0 + sc1) + sh1
    proj = _dot(hn.astype(BF16), win_ref[...])
    c0 = conv_ch
    cb, cc, ch = proj[:, 0:c0], proj[:, c0:2 * c0], proj[:, 2 * c0:3 * c0]
    o0 = 3 * c0
    q_raw = proj[:, o0:o0 + hgrn_w]
    f_raw = proj[:, o0 + hgrn_w:o0 + 2 * hgrn_w]
    v_all = proj[:, o0 + 2 * hgrn_w:o0 + 3 * hgrn_w]
    og = proj[:, o0 + 3 * hgrn_w:o0 + 4 * hgrn_w]

    u = cc * ch
    carry = carry_ref[...]
    row8 = lax.broadcasted_iota(I32, (SUBLANES, c0), 0)
    u1 = pltpu.roll(u, 1, 0)
    u2 = pltpu.roll(u, 2, 0)
    u1 = jnp.concatenate([jnp.where(row8 < 1, pltpu.roll(carry, 1, 0), u1[0:SUBLANES]), u1[SUBLANES:]], axis=0)
    u2 = jnp.concatenate([jnp.where(row8 < 2, pltpu.roll(carry, 2, 0), u2[0:SUBLANES]), u2[SUBLANES:]], axis=0)
    carry_ref[...] = u[t_rows - SUBLANES:t_rows]
    yc = cb * (convw_ref[0:1, :] * u2 + convw_ref[1:2, :] * u1 + convw_ref[2:3, :] * u)
    sq = yc * yc
    sq_hi = sq.astype(BF16)
    sq_lo = (sq - sq_hi.astype(F32)).astype(BF16)
    gms = _dot(sq_hi, gmean_ref[...]) + _dot(sq_lo, gmean_ref[...])
    ycn = yc * lax.rsqrt(gms + RMS_EPS) * cng_ref[...]

    lb = lb_ref[...]
    log_lb = jnp.log(lb)
    log_1mlb = jnp.log1p(-lb)
    log_sig = jnp.minimum(f_raw, 0.0) - jnp.log(1.0 + jnp.exp(-jnp.abs(f_raw)))
    bb = log_1mlb + log_sig
    log_f = jnp.maximum(log_lb, bb) + jnp.log(1.0 + jnp.exp(-jnp.abs(log_lb - bb)))
    k_all = 1.0 - jnp.exp(log_f)
    q_all = q_raw * _sigmoid(q_raw)
    roww = lax.broadcasted_iota(I32, (t_rows, hgrn_w), 0)
    g_cum = log_f
    s = 1
    while s < t_rows:
        g_cum = g_cum + jnp.where(roww >= s, pltpu.roll(g_cum, s, 0), 0.0)
        s *= 2
    small = _small_level_exponents(log_f, roww)
    row = lax.broadcasted_iota(I32, (t_rows, HEAD_DIM), 0)
    rowsq = lax.broadcasted_iota(I32, (t_rows, t_rows), 0)
    col = lax.broadcasted_iota(I32, (t_rows, t_rows), 1)
    group_masks = {}
    grp = 2
    while grp < t_rows:
        shift = grp.bit_length() - 1
        group_masks[grp] = lax.shift_right_logical(rowsq, shift) == lax.shift_right_logical(col, shift)
        grp *= 2
    outs = []
    for h in range(heads):
        sl = slice(h * HEAD_DIM, (h + 1) * HEAD_DIM)
        small_h = {lvl: a[:, sl] for lvl, a in small.items()}
        o = _hgrn_head(q_all[:, sl], k_all[:, sl], v_all[:, sl], g_cum[:, sl], small_h, group_masks, st_ref, h,
                       row)
        outs.append(_rms(o))
    oh = jnp.concatenate(outs, axis=1) * hng_ref[...] * (og * _sigmoid(og))

    mix = _dot(ycn.astype(BF16), wout_ref[0:c0, :]) + _dot(oh.astype(BF16), wout_ref[c0:c0 + hgrn_w, :])
    xn = x + g1 * mix
    xo_ref[0] = xn

    hn2 = (_rms(xn) * n2g_ref[...]) * (1.0 + sc2) + sh2
    dh = hn2.shape[1] // 2
    hn2_ref[...] = _pack_bf16_pair(hn2[:, 0:dh], hn2[:, dh:2 * dh])
    h_hi = hn2.astype(BF16)
    h_lo = (hn2 - h_hi.astype(F32)).astype(BF16)
    prod = _dot(h_hi, wrt_ref[...])
    logits_tok = prod[:, 0:128] + prod[:, 128:256] + _dot(h_lo, wrt_ref[:, 0:128])
    logits = logits_tok.T[0:N_EXPERTS, :] + br_ref[:, 0:1]
    eio = lax.broadcasted_iota(I32, logits.shape, 0)
    order = jnp.zeros(logits.shape, F32)
    for e2 in range(N_EXPERTS):
        other = logits[e2:e2 + 1, :]
        beats = (other > logits) | ((other == logits) & (eio > e2))
        order = order + beats.astype(F32)
    eio_f = eio.astype(F32)
    vals, sels = [], []
    for kk in range(TOP_K):
        sel = order == float(kk)
        vals.append(jnp.sum(jnp.where(sel, logits, 0.0), axis=0, keepdims=True))
        sels.append(sel)
        eidx_ref[kk:kk + 1, :] = jnp.sum(jnp.where(sel, eio_f, 0.0), axis=0, keepdims=True).astype(I32)
    exps = [jnp.exp(vv - vals[0]) for vv in vals]
    tot = functools.reduce(lambda a, c: a + c, exps)
    member = functools.reduce(lambda a, c: a + c, [s_.astype(F32) for s_ in sels])
    before = _dot(member.astype(BF16), upper_ref[...])
    base = base_ref[:, 0:1]
    slot = before + base
    for kk in range(TOP_K):
        gate_ref[kk:kk + 1, :] = exps[kk] / tot
        rank_ref[kk:kk + 1, :] = jnp.sum(jnp.where(sels[kk], slot, 0.0), axis=0, keepdims=True).astype(I32)
    base_new = base + jnp.sum(member, axis=1, keepdims=True)
    base_ref[...] = jnp.broadcast_to(base_new, base_ref.shape)
    cnt_ref[...] = jnp.broadcast_to(base_new, cnt_ref.shape)


def _mixer_call(x, ada_l, n1g, win, convw, cng, lb, hng, wout, gmean, n2g, wrt, br, upper):
    bsz, seq, d = x.shape
    t = MIX_ROWS
    n_tok = bsz * seq
    conv_ch = convw.shape[1]
    hgrn_w = lb.shape[1]
    heads = hgrn_w // HEAD_DIM
    steps = seq // t
    full = lambda a: pl.BlockSpec(a.shape, lambda b, i: (0,) * a.ndim)
    tokmap = lambda b, i: (0, b * steps + i)
    kern = functools.partial(_mixer_kernel, conv_ch=conv_ch, hgrn_w=hgrn_w)
    return pl.pallas_call(
        kern,
        grid=(bsz, steps),
        in_specs=[
            pl.BlockSpec((1, t, d), lambda b, i: (b, i, 0)),
            pl.BlockSpec((1, 6, d), lambda b, i: (b, 0, 0)),
            full(n1g), full(win), full(convw), full(cng), full(lb), full(hng), full(wout),
            full(gmean), full(n2g), full(wrt), full(br), full(upper),
        ],
        out_specs=[
            pl.BlockSpec((1, t, d), lambda b, i: (b, i, 0)),
            pl.BlockSpec((t, d // 2), lambda b, i: (b * steps + i, 0)),
            pl.BlockSpec((TOP_K, t), tokmap),
            pl.BlockSpec((TOP_K, t), tokmap),
            pl.BlockSpec((TOP_K, t), tokmap),
            pl.BlockSpec((N_EXPERTS, 128), lambda b, i: (0, 0)),
        ],
        out_shape=[
            jax.ShapeDtypeStruct((bsz, seq, d), F32),
            jax.ShapeDtypeStruct((n_tok, d // 2), I32),
            jax.ShapeDtypeStruct((TOP_K, n_tok), I32),
            jax.ShapeDtypeStruct((TOP_K, n_tok), F32),
            jax.ShapeDtypeStruct((TOP_K, n_tok), I32),
            jax.ShapeDtypeStruct((N_EXPERTS, 128), F32),
        ],
        scratch_shapes=[
            pltpu.VMEM((heads, HEAD_DIM, HEAD_DIM), F32),
            pltpu.VMEM((SUBLANES, conv_ch), F32),
            pltpu.VMEM((N_EXPERTS, 128), F32),
        ],
        compiler_params=pltpu.CompilerParams(
            dimension_semantics=("arbitrary", "arbitrary"), vmem_limit_bytes=VMEM_LIMIT),
        name="token_mixer",
    )(x, ada_l, n1g, win, convw, cng, lb, hng, wout, gmean, n2g, wrt, br, upper)


def _sc_worker():
    return lax.axis_index("s") * SC_CORES + lax.axis_index("c")


def _sc_mesh():
    return plsc.VectorSubcoreMesh(core_axis_name="c", subcore_axis_name="s")


def _sc_scatter_rows(src, pos3d, p_rows):
    kk, n_idx_rows, ch = pos3d.shape
    d = src.shape[1]
    n_ch = n_idx_rows // SC_WORKERS
    assert n_idx_rows % SC_WORKERS == 0 and n_ch % 2 == 0

    @functools.partial(
        pl.kernel, mesh=_sc_mesh(), out_type=jax.ShapeDtypeStruct((p_rows, d), src.dtype),
        scratch_types=[pltpu.VMEM((kk, n_ch, ch), I32), pltpu.VMEM((2, ch, d), src.dtype),
                       pltpu.SemaphoreType.DMA((2,)), pltpu.SemaphoreType.DMA((2,))],
        name="sc_scatter_rows")
    def scatter_kernel(src_hbm, pos_hbm, out_hbm, idx_v, rows_v, lsem, ssem):
        wid = _sc_worker()
        for j in range(kk):
            pltpu.sync_copy(pos_hbm.at[j, pl.ds(wid * n_ch, n_ch)], idx_v.at[j])
        base = wid * (n_ch * ch)

        def load(c, slot):
            return pltpu.make_async_copy(src_hbm.at[pl.ds(base + c * ch, ch)], rows_v.at[slot], lsem.at[slot])

        def scatter(c, slot, j):
            return pltpu.make_async_copy(rows_v.at[slot], out_hbm.at[idx_v.at[j, c]], ssem.at[slot])

        load(0, 0).start()

        @pl.loop(0, n_ch, step=2)
        def _(c0):
            for slot in range(2):
                c = c0 + slot
                other = 1 - slot

                @pl.when(c >= 1)
                def _():
                    for j in range(kk):
                        scatter(c - 1, other, j).wait()

                @pl.when(c + 1 < n_ch)
                def _():
                    load(c + 1, other).start()

                load(c, slot).wait()
                for j in range(kk):
                    scatter(c, slot, j).start()

        for j in range(kk):
            scatter(n_ch - 1, (n_ch - 1) % 2, j).wait()

    return scatter_kernel(src, pos3d)


def _sc_gather_rows(table, idx2d):
    n_idx_rows, ch = idx2d.shape
    d = table.shape[1]
    n_ch = n_idx_rows // SC_WORKERS
    assert n_idx_rows % SC_WORKERS == 0 and n_ch % 2 == 0

    @functools.partial(
        pl.kernel, mesh=_sc_mesh(), out_type=jax.ShapeDtypeStruct((n_idx_rows * ch, d), table.dtype),
        scratch_types=[pltpu.VMEM((n_ch, ch), I32), pltpu.VMEM((2, ch, d), table.dtype),
                       pltpu.SemaphoreType.DMA((2,)), pltpu.SemaphoreType.DMA((2,))],
        name="sc_gather_rows")
    def gather_kernel(table_hbm, idx_hbm, out_hbm, idx_v, rows_v, gsem, wsem):
        wid = _sc_worker()
        pltpu.sync_copy(idx_hbm.at[pl.ds(wid * n_ch, n_ch)], idx_v)
        base = wid * (n_ch * ch)

        def gather(c, slot):
            return pltpu.make_async_copy(table_hbm.at[idx_v.at[c]], rows_v.at[slot], gsem.at[slot])

        def write(c, slot):
            return pltpu.make_async_copy(rows_v.at[slot], out_hbm.at[pl.ds(base + c * ch, ch)], wsem.at[slot])

        gather(0, 0).start()

        @pl.loop(0, n_ch, step=2)
        def _(c0):
            for slot in range(2):
                c = c0 + slot
                other = 1 - slot

                @pl.when(c >= 1)
                def _():
                    write(c - 1, other).wait()

                @pl.when(c + 1 < n_ch)
                def _():
                    gather(c + 1, other).start()

                gather(c, slot).wait()
                write(c, slot).start()

        write(n_ch - 1, (n_ch - 1) % 2).wait()

    return gather_kernel(table, idx2d)


def _expert_kernel(eid_ref, nvalid_ref, first_ref, next_ref, slot_ref, xs_ref, wgu_hbm, bgu_ref, wdn_hbm, bdn_ref,
                   ys_ref, wgu_st, wdn_st, wgu_bf, wdn_bf, sem_gu, sem_dn, *, layer):
    i = pl.program_id(0)
    nvalid = nvalid_ref[i]
    active = nvalid > 0

    def weight_copies(expert, slot):
        return (pltpu.make_async_copy(wgu_hbm.at[layer, expert], wgu_st.at[slot], sem_gu.at[slot]),
                pltpu.make_async_copy(wdn_hbm.at[layer, expert], wdn_st.at[slot], sem_dn.at[slot]))

    @pl.when(i == 0)
    def _():
        for cp in weight_copies(eid_ref[0], slot_ref[0]):
            cp.start()

    @pl.when(first_ref[i] == 1)
    def _():
        slot = slot_ref[i]
        for cp in weight_copies(eid_ref[i], slot):
            cp.wait()

        @pl.when(next_ref[i] >= 0)
        def _():
            for cp in weight_copies(next_ref[i], 1 - slot):
                cp.start()

        def cast_rows(r, carry):
            rows = pl.ds(pl.multiple_of(r * WEIGHT_CAST_ROWS, WEIGHT_CAST_ROWS), WEIGHT_CAST_ROWS)
            wgu_bf[rows, :] = wgu_st[slot, rows, :].astype(BF16)
            wdn_bf[rows, :] = wdn_st[slot, rows, :].astype(BF16)
            return carry
        lax.fori_loop(0, wgu_bf.shape[0] // WEIGHT_CAST_ROWS, cast_rows, 0)

    @pl.when(active)
    def _():
        packed = xs_ref[...]
        keep = lax.broadcasted_iota(I32, packed.shape, 0) < nvalid
        lo, hi = _unpack_bf16_pair(jnp.where(keep, packed, 0))
        dh = lo.shape[1]
        hgu = (_dot(lo.astype(BF16), wgu_bf[0:dh, :]) + _dot(hi.astype(BF16), wgu_bf[dh:2 * dh, :])
               + bgu_ref[0, 0])
        f = hgu.shape[1] // 2
        a = jnp.minimum(hgu[:, 0:f], SWIGLU_LIMIT)
        g = jnp.clip(hgu[:, f:2 * f], -SWIGLU_LIMIT, SWIGLU_LIMIT)
        act = a * _sigmoid(SWIGLU_ALPHA * a) * (g + 1.0)
        y = _dot(act.astype(BF16), wdn_bf[...]) + bdn_ref[0, 0]
        do = y.shape[1] // 2
        ys_ref[...] = _pack_bf16_pair(y[:, 0:do], y[:, do:2 * do])

    @pl.when(jnp.logical_not(active))
    def _():
        ys_ref[...] = jnp.zeros_like(ys_ref)


def _expert_call(layer, tables, xs, wgu, bgu, wdn, bdn):
    p_rows, dh = xs.shape
    bm = EXPERT_ROWS
    depth, n_e, d, f2 = wgu.shape
    dout = wdn.shape[3]
    assert f2 // 2 == d and d % WEIGHT_CAST_ROWS == 0
    bias_map = lambda i, eid, *_: (layer, eid[i], 0, 0)
    grid_spec = pltpu.PrefetchScalarGridSpec(
        num_scalar_prefetch=len(tables),
        grid=(p_rows // bm,),
        in_specs=[
            pl.BlockSpec((bm, dh), lambda i, *_: (i, 0)),
            pl.BlockSpec(memory_space=pl.ANY),
            pl.BlockSpec((1, 1, 1, f2), bias_map),
            pl.BlockSpec(memory_space=pl.ANY),
            pl.BlockSpec((1, 1, 1, dout), bias_map),
        ],
        out_specs=pl.BlockSpec((bm, dout // 2), lambda i, *_: (i, 0)),
        scratch_shapes=[
            pltpu.VMEM((2, d, f2), F32), pltpu.VMEM((2, f2 // 2, dout), F32),
            pltpu.VMEM((d, f2), BF16), pltpu.VMEM((f2 // 2, dout), BF16),
            pltpu.SemaphoreType.DMA((2,)), pltpu.SemaphoreType.DMA((2,)),
        ],
    )
    return pl.pallas_call(
        functools.partial(_expert_kernel, layer=layer),
        grid_spec=grid_spec,
        out_shape=jax.ShapeDtypeStruct((p_rows, dout // 2), I32),
        compiler_params=pltpu.CompilerParams(
            dimension_semantics=("arbitrary",), vmem_limit_bytes=VMEM_LIMIT),
        name="moe_experts",
    )(*tables, xs, wgu, bgu.reshape(depth, n_e, 1, f2), wdn, bdn.reshape(depth, n_e, 1, dout))


def _combine_kernel(x_ref, gate_ref, g2_ref, fg_ref, yg_ref, o_ref, *, final_norm):
    t_rows, d = x_ref.shape
    dh = d // 2
    acc_lo = jnp.zeros((t_rows, dh), F32)
    acc_hi = jnp.zeros((t_rows, dh), F32)
    for kk in range(TOP_K):
        lo, hi = _unpack_bf16_pair(yg_ref[kk])
        gk = gate_ref[:, kk:kk + 1]
        acc_lo = acc_lo + gk * lo
        acc_hi = acc_hi + gk * hi
    out = x_ref[...] + g2_ref[...] * jnp.concatenate([acc_lo, acc_hi], axis=1)
    if final_norm:
        out = _rms(out) * fg_ref[...]
    o_ref[...] = out


def _combine_call(x2d, gates_t, g2, final_g, yg, final_norm):
    n_tok, d = x2d.shape
    t = COMBINE_ROWS
    kern = functools.partial(_combine_kernel, final_norm=final_norm)
    return pl.pallas_call(
        kern,
        grid=(n_tok // t,),
        in_specs=[
            pl.BlockSpec((t, d), lambda i: (i, 0)),
            pl.BlockSpec((t, TOP_K), lambda i: (i, 0)),
            pl.BlockSpec((1, d), lambda i: (0, 0)),
            pl.BlockSpec((1, d), lambda i: (0, 0)),
            pl.BlockSpec((TOP_K, t, d // 2), lambda i: (0, i, 0)),
        ],
        out_specs=pl.BlockSpec((t, d), lambda i: (i, 0)),
        out_shape=jax.ShapeDtypeStruct((n_tok, d), F32),
        compiler_params=pltpu.CompilerParams(
            dimension_semantics=("arbitrary",), vmem_limit_bytes=VMEM_LIMIT),
        name="moe_combine",
    )(x2d, gates_t, g2, final_g, yg)


def _route_tables(eidx, rank, cnt, n_blocks):
    bm = EXPERT_ROWS
    counts = cnt[:, 0].astype(I32)
    padded = ((counts + bm - 1) // bm) * bm
    pends = jnp.cumsum(padded)
    pstarts = pends - padded
    pos = rank
    for e in range(N_EXPERTS):
        pos = pos + jnp.where(eidx == e, pstarts[e], 0)
    block_row = jnp.arange(n_blocks, dtype=I32) * bm
    block_eid = jnp.minimum(jnp.sum((pends[None, :] <= block_row[:, None]).astype(I32), axis=1), N_EXPERTS - 1)
    experts = jnp.arange(N_EXPERTS, dtype=I32)
    onehot = block_eid[:, None] == experts[None, :]
    row_end = jnp.sum(jnp.where(onehot, (pstarts + counts)[None, :], 0), axis=1)
    block_valid = jnp.clip(row_end - block_row, 0, bm).astype(I32)
    active = block_valid > 0
    prev_eid = jnp.concatenate([jnp.full((1,), -1, I32), block_eid[:-1]])
    first = (active & (block_eid != prev_eid)).astype(I32)
    later = (experts[None, :] > experts[:, None]) & (counts[None, :] > 0)
    next_expert = jnp.min(jnp.where(later, experts[None, :], N_EXPERTS), axis=1)
    next_expert = jnp.where(next_expert == N_EXPERTS, -1, next_expert)
    block_next = jnp.sum(jnp.where(onehot, next_expert[None, :], 0), axis=1).astype(I32)
    slot = ((jnp.cumsum(first) - 1) % 2).astype(I32)
    return pos, (block_eid, block_valid, first, block_next, slot)


def _moe_rows(layer, hn2p, eidx, rank, cnt, wgu, bgu, wdn, bdn):
    n_tok, dh = hn2p.shape
    n_blocks = n_tok * TOP_K // EXPERT_ROWS + N_EXPERTS
    pos, tables = _route_tables(eidx, rank, cnt, n_blocks)
    xs = _sc_scatter_rows(hn2p, pos.reshape(TOP_K, n_tok // SC_CHUNK, SC_CHUNK), n_blocks * EXPERT_ROWS)
    ys = _expert_call(layer, tables, xs, wgu, bgu, wdn, bdn)
    yg = _sc_gather_rows(ys, pos.reshape(TOP_K * n_tok // SC_CHUNK, SC_CHUNK))
    return yg.reshape(TOP_K, n_tok, dh)


def kernel(x, c, w_ada, b_ada, norm1_g, norm2_g, w_in, conv_w, conv_norm_g, lower_bounds, hgrn_norm_g, w_out,
           w_router, b_router, w_gu, b_gu, w_down, b_down, final_g):
    depth = w_ada.shape[0]
    bsz, seq, d = x.shape
    conv_ch = conv_w.shape[2]
    c_pad = jnp.zeros((SUBLANES, d), F32).at[:bsz].set(c)
    ada = _ada_call(c_pad, w_ada, b_ada)[:, :bsz].reshape(depth, bsz, 6, d)
    lb_all = _bounds_call(lower_bounds)

    ci = jnp.arange(conv_ch)
    gmean = jnp.where((ci[:, None] // CONV_GROUP) == (ci[None, :] // CONV_GROUP), 1.0 / CONV_GROUP, 0.0).astype(BF16)
    ti = jnp.arange(MIX_ROWS)
    upper = (ti[:, None] < ti[None, :]).astype(BF16)

    xb = [x[b:b + 1] for b in range(bsz)]
    for l in range(depth):
        win, wout = w_in[l].astype(BF16), w_out[l].astype(BF16)
        wr_hi = w_router[l].astype(BF16)
        wr_lo = (w_router[l] - wr_hi.astype(F32)).astype(BF16)
        wrt = jnp.zeros((d, 256), BF16).at[:, :N_EXPERTS].set(wr_hi).at[:, 128:128 + N_EXPERTS].set(wr_lo)
        br = jnp.broadcast_to(b_router[l][:, None], (N_EXPERTS, 128))
        mixed = [
            _mixer_call(xb[b], ada[l, b:b + 1], norm1_g[l][None], win, conv_w[l], conv_norm_g[l][None],
                        lb_all[l][None], hgrn_norm_g[l][None], wout, gmean, norm2_g[l][None], wrt, br, upper)
            for b in range(bsz)]
        for b in range(bsz):
            xn, hn2p, eidx, gates, rank, cnt = mixed[b]
            yg = _moe_rows(l, hn2p, eidx, rank, cnt, w_gu, b_gu, w_down, b_down)
            xb[b] = _combine_call(xn.reshape(seq, d), gates.T, ada[l, b, 5:6, :], final_g[None], yg,
                                  final_norm=(l == depth - 1)).reshape(1, seq, d)
    return jnp.concatenate(xb, axis=0)
```

```python
import functools

import jax
import jax.numpy as jnp
from jax import lax
from jax.experimental import pallas as pl
from jax.experimental.pallas import tpu as pltpu
from jax.experimental.pallas import tpu_sc as plsc

F32 = jnp.float32
BF16 = jnp.bfloat16
I32 = jnp.int32
HIGHEST = lax.Precision.HIGHEST

RMS_EPS = 1e-6
N_EXPERTS = 32
TOP_K = 4
CONV_GROUP = 64
HEAD_DIM = 128
SWIGLU_LIMIT = 7.0
SWIGLU_ALPHA = 1.702
SUBLANES = 8
VMEM_LIMIT = 56 * 1024 * 1024

MIX_ROWS = 256
EXPERT_ROWS = 512
COMBINE_ROWS = 512
SC_CORES = 2
SC_SUBCORES = 16
SC_WORKERS = SC_CORES * SC_SUBCORES
SC_CHUNK = 64
ADA_COLS = 1536
WEIGHT_CAST_ROWS = 128


def _dot(a, b):
    return jnp.dot(a, b, preferred_element_type=F32)


def _dot_nt(a, b, precision=None):
    return lax.dot_general(a, b, (((1,), (1,)), ((), ())), precision=precision,
                           preferred_element_type=F32)


def _dot_tn(a, b):
    return lax.dot_general(a, b, (((0,), (0,)), ((), ())), preferred_element_type=F32)


def _sigmoid(x):
    return 1.0 / (1.0 + jnp.exp(-x))


def _rms(x):
    return x * lax.rsqrt(jnp.mean(x * x, axis=-1, keepdims=True) + RMS_EPS)


def _pack_bf16_pair(lo, hi):
    lo_bits = lax.bitcast_convert_type(lo.astype(BF16).astype(F32), I32)
    hi_bits = lax.bitcast_convert_type(hi.astype(BF16).astype(F32), I32)
    return lax.shift_right_logical(lo_bits, 16) | (hi_bits & jnp.int32(-65536))


def _unpack_bf16_pair(p):
    lo = lax.bitcast_convert_type(lax.shift_left(p, 16), F32)
    hi = lax.bitcast_convert_type(p & jnp.int32(-65536), F32)
    return lo, hi


def _ada_kernel(c_ref, w_ref, b_ref, o_ref):
    cv = c_ref[...]
    ca = cv * _sigmoid(cv)
    o_ref[0] = jnp.dot(ca, w_ref[0], precision=HIGHEST, preferred_element_type=F32) + b_ref[0]


def _ada_call(c_pad, w_ada, b_ada):
    depth, d, n6 = w_ada.shape
    rows = c_pad.shape[0]
    return pl.pallas_call(
        _ada_kernel,
        grid=(depth, n6 // ADA_COLS),
        in_specs=[
            pl.BlockSpec((rows, d), lambda l, j: (0, 0)),
            pl.BlockSpec((1, d, ADA_COLS), lambda l, j: (l, 0, j)),
            pl.BlockSpec((1, 1, ADA_COLS), lambda l, j: (l, 0, j)),
        ],
        out_specs=pl.BlockSpec((1, rows, ADA_COLS), lambda l, j: (l, 0, j)),
        out_shape=jax.ShapeDtypeStruct((depth, rows, n6), F32),
        compiler_params=pltpu.CompilerParams(
            dimension_semantics=("arbitrary", "arbitrary"), vmem_limit_bytes=VMEM_LIMIT),
        name="ada_proj",
    )(c_pad, w_ada, b_ada.reshape(depth, 1, n6))


def _bounds_kernel(lb_ref, o_ref):
    rows = [lb_ref[l:l + 1, :] for l in range(lb_ref.shape[0])]
    m = functools.reduce(jnp.maximum, rows)
    es = [jnp.exp(r - m) for r in rows]
    tot = functools.reduce(lambda a, b: a + b, es)
    cum = None
    first = None
    for l, e in enumerate(es):
        p = e / tot
        cum = p if cum is None else cum + p
        if first is None:
            first = cum
        o_ref[l:l + 1, :] = cum - first


def _bounds_call(lower_bounds):
    return pl.pallas_call(
        _bounds_kernel,
        out_shape=jax.ShapeDtypeStruct(lower_bounds.shape, F32),
        name="hgrn_bounds",
    )(lower_bounds.astype(F32))


def _small_level_exponents(g, row):
    t_rows, width = g.shape
    blocks = t_rows // SUBLANES

    def blockroll(x, k):
        return pltpu.roll(x.reshape(blocks, SUBLANES, width), k, 1).reshape(t_rows, width)

    a1 = g + blockroll(g, 1)
    a2 = a1 + blockroll(g, 2)
    a3 = a2 + blockroll(g, 3)
    b1 = blockroll(g, SUBLANES - 1)
    b2 = b1 + blockroll(g, SUBLANES - 2)
    b3 = b2 + blockroll(g, SUBLANES - 3)
    o = row & 7
    o2 = row & 3
    arg1 = jnp.where((row & 1) != 0, g, 0.0)
    arg2 = jnp.where(o2 == 3, a1, jnp.where(o2 == 2, g, jnp.where(o2 == 0, b1, 0.0)))
    up4 = jnp.where(o == 7, a3, jnp.where(o == 6, a2, jnp.where(o == 5, a1, g)))
    lo4 = jnp.where(o == 0, b3, jnp.where(o == 1, b2, jnp.where(o == 2, b1, 0.0)))
    arg4 = jnp.where(o >= 4, up4, lo4)
    return {1: arg1, 2: arg2, 4: arg4}


def _hgrn_head(q, k, v, g_cum, small_args, group_masks, st_ref, h, row):
    t_rows = q.shape[0]
    scores = jnp.zeros((t_rows, t_rows), F32)
    half = t_rows // 2
    while half >= 1:
        grp = 2 * half
        upper = (row & half) != 0
        if half >= SUBLANES:
            g_mid = g_cum.reshape(t_rows // grp, grp, HEAD_DIM)[:, half - 1:half, :]
            g_mid = jnp.broadcast_to(g_mid, (t_rows // grp, grp, HEAD_DIM)).reshape(t_rows, HEAD_DIM)
            d = g_cum - g_mid
            arg = jnp.where(upper, d, -d)
        else:
            arg = small_args[half]
        e = jnp.exp(arg)
        qh = jnp.where(upper, q * e, 0.0).astype(BF16)
        kh = jnp.where(upper, 0.0, k * e).astype(BF16)
        sc = _dot_nt(qh, kh)
        if grp < t_rows:
            sc = jnp.where(group_masks[grp], sc, 0.0)
        scores = scores + sc
        half //= 2

    out = _dot(scores.astype(BF16), v.astype(BF16)) + jnp.sum(q * k, axis=-1, keepdims=True) * v

    st = st_ref[h]
    out = out + _dot_nt((q * jnp.exp(g_cum)).astype(BF16), st.astype(BF16))
    g_last = g_cum[t_rows - 1:t_rows, :]
    kd = (k * jnp.exp(g_last - g_cum)).astype(BF16)
    st_ref[h] = st * jnp.exp(g_last) + _dot_tn(v.astype(BF16), kd)
    return out


def _mixer_kernel(x_ref, ada_ref, n1g_ref, win_ref, convw_ref, cng_ref, lb_ref, hng_ref, wout_ref,
                  gmean_ref, n2g_ref, wrt_ref, br_ref, upper_ref, after_ref,
                  xo_ref, hn2_ref, eidx_ref, gate_ref, rank_ref, cnt_ref,
                  st_ref, carry_ref, base_ref, *, conv_ch, hgrn_w):
    del after_ref
    b = pl.program_id(0)
    i = pl.program_id(1)
    t_rows = x_ref.shape[1]
    heads = hgrn_w // HEAD_DIM

    @pl.when(i == 0)
    def _():
        st_ref[...] = jnp.zeros_like(st_ref)
        carry_ref[...] = jnp.zeros_like(carry_ref)

    @pl.when((b == 0) & (i == 0))
    def _():
        base_ref[...] = jnp.zeros_like(base_ref)

    sh1, sc1, g1 = ada_ref[0, 0:1, :], ada_ref[0, 1:2, :], ada_ref[0, 2:3, :]
    sh2, sc2, g2 = ada_ref[0, 3:4, :], ada_ref[0, 4:5, :], ada_ref[0, 5:6, :]
    del g2

    x = x_ref[0]
    hn = (_rms(x) * n1g_ref[...]) * (1.0 + sc1) + sh1
    proj = _dot(hn.astype(BF16), win_ref[...])
    c0 = conv_ch
    cb, cc, ch = proj[:, 0:c0], proj[:, c0:2 * c0], proj[:, 2 * c0:3 * c0]
    o0 = 3 * c0
    q_raw = proj[:, o0:o0 + hgrn_w]
    f_raw = proj[:, o0 + hgrn_w:o0 + 2 * hgrn_w]
    v_all = proj[:, o0 + 2 * hgrn_w:o0 + 3 * hgrn_w]
    og = proj[:, o0 + 3 * hgrn_w:o0 + 4 * hgrn_w]

    u = cc * ch
    carry = carry_ref[...]
    row8 = lax.broadcasted_iota(I32, (SUBLANES, c0), 0)
    u1 = pltpu.roll(u, 1, 0)
    u2 = pltpu.roll(u, 2, 0)
    u1 = jnp.concatenate([jnp.where(row8 < 1, pltpu.roll(carry, 1, 0), u1[0:SUBLANES]), u1[SUBLANES:]], axis=0)
    u2 = jnp.concatenate([jnp.where(row8 < 2, pltpu.roll(carry, 2, 0), u2[0:SUBLANES]), u2[SUBLANES:]], axis=0)
    carry_ref[...] = u[t_rows - SUBLANES:t_rows]
    yc = cb * (convw_ref[0:1, :] * u2 + convw_ref[1:2, :] * u1 + convw_ref[2:3, :] * u)
    sq = yc * yc
    sq_hi = sq.astype(BF16)
    sq_lo = (sq - sq_hi.astype(F32)).astype(BF16)
    gms = _dot(sq_hi, gmean_ref[...]) + _dot(sq_lo, gmean_ref[...])
    ycn = yc * lax.rsqrt(gms + RMS_EPS) * cng_ref[...]

    lb = lb_ref[...]
    log_lb = jnp.log(lb)
    log_1mlb = jnp.log1p(-lb)
    log_sig = jnp.minimum(f_raw, 0.0) - jnp.log(1.0 + jnp.exp(-jnp.abs(f_raw)))
    bb = log_1mlb + log_sig
    log_f = jnp.maximum(log_lb, bb) + jnp.log(1.0 + jnp.exp(-jnp.abs(log_lb - bb)))
    k_all = 1.0 - jnp.exp(log_f)
    q_all = q_raw * _sigmoid(q_raw)
    roww = lax.broadcasted_iota(I32, (t_rows, hgrn_w), 0)
    g_cum = log_f
    s = 1
    while s < t_rows:
        g_cum = g_cum + jnp.where(roww >= s, pltpu.roll(g_cum, s, 0), 0.0)
        s *= 2
    small = _small_level_exponents(log_f, roww)
    row = lax.broadcasted_iota(I32, (t_rows, HEAD_DIM), 0)
    rowsq = lax.broadcasted_iota(I32, (t_rows, t_rows), 0)
    col = lax.broadcasted_iota(I32, (t_rows, t_rows), 1)
    group_masks = {}
    grp = 2
    while grp < t_rows:
        shift = grp.bit_length() - 1
        group_masks[grp] = lax.shift_right_logical(rowsq, shift) == lax.shift_right_logical(col, shift)
        grp *= 2
    outs = []
    for h in range(heads):
        sl = slice(h * HEAD_DIM, (h + 1) * HEAD_DIM)
        small_h = {lvl: a[:, sl] for lvl, a in small.items()}
        o = _hgrn_head(q_all[:, sl], k_all[:, sl], v_all[:, sl], g_cum[:, sl], small_h, group_masks, st_ref, h,
                       row)
        outs.append(_rms(o))
    oh = jnp.concatenate(outs, axis=1) * hng_ref[...] * (og * _sigmoid(og))

    mix = _dot(ycn.astype(BF16), wout_ref[0:c0, :]) + _dot(oh.astype(BF16), wout_ref[c0:c0 + hgrn_w, :])
    xn = x + g1 * mix
    xo_ref[0] = xn

    hn2 = (_rms(xn) * n2g_ref[...]) * (1.0 + sc2) + sh2
    dh = hn2.shape[1] // 2
    hn2_ref[...] = _pack_bf16_pair(hn2[:, 0:dh], hn2[:, dh:2 * dh])
    h_hi = hn2.astype(BF16)
    h_lo = (hn2 - h_hi.astype(F32)).astype(BF16)
    prod = _dot(h_hi, wrt_ref[...])
    logits_tok = prod[:, 0:128] + prod[:, 128:256] + _dot(h_lo, wrt_ref[:, 0:128])
    logits = logits_tok.T[0:N_EXPERTS, :] + br_ref[:, 0:1]
    eio = lax.broadcasted_iota(I32, logits.shape, 0)
    order = jnp.zeros(logits.shape, F32)
    for e2 in range(N_EXPERTS):
        other = logits[e2:e2 + 1, :]
        beats = (other > logits) | ((other == logits) & (eio > e2))
        order = order + beats.astype(F32)
    eio_f = eio.astype(F32)
    vals, sels = [], []
    for kk in range(TOP_K):
        sel = order == float(kk)
        vals.append(jnp.sum(jnp.where(sel, logits, 0.0), axis=0, keepdims=True))
        sels.append(sel)
        eidx_ref[kk:kk + 1, :] = jnp.sum(jnp.where(sel, eio_f, 0.0), axis=0, keepdims=True).astype(I32)
    exps = [jnp.exp(vv - vals[0]) for vv in vals]
    tot = functools.reduce(lambda a, c: a + c, exps)
    member = functools.reduce(lambda a, c: a + c, [s_.astype(F32) for s_ in sels])
    before = _dot(member.astype(BF16), upper_ref[...])
    base = base_ref[:, 0:1]
    slot = before + base
    for kk in range(TOP_K):
        gate_ref[kk:kk + 1, :] = exps[kk] / tot
        rank_ref[kk:kk + 1, :] = jnp.sum(jnp.where(sels[kk], slot, 0.0), axis=0, keepdims=True).astype(I32)
    base_new = base + jnp.sum(member, axis=1, keepdims=True)
    base_ref[...] = jnp.broadcast_to(base_new, base_ref.shape)
    cnt_ref[...] = jnp.broadcast_to(base_new, cnt_ref.shape)


def _mixer_call(x, ada_l, n1g, win, convw, cng, lb, hng, wout, gmean, n2g, wrt, br, upper, after):
    bsz, seq, d = x.shape
    t = MIX_ROWS
    n_tok = bsz * seq
    conv_ch = convw.shape[1]
    hgrn_w = lb.shape[1]
    heads = hgrn_w // HEAD_DIM
    steps = seq // t
    full = lambda a: pl.BlockSpec(a.shape, lambda b, i: (0,) * a.ndim)
    tokmap = lambda b, i: (0, b * steps + i)
    kern = functools.partial(_mixer_kernel, conv_ch=conv_ch, hgrn_w=hgrn_w)
    return pl.pallas_call(
        kern,
        grid=(bsz, steps),
        in_specs=[
            pl.BlockSpec((1, t, d), lambda b, i: (b, i, 0)),
            pl.BlockSpec((1, 6, d), lambda b, i: (b, 0, 0)),
            full(n1g), full(win), full(convw), full(cng), full(lb), full(hng), full(wout),
            full(gmean), full(n2g), full(wrt), full(br), full(upper),
            pl.BlockSpec(memory_space=pl.ANY),
        ],
        out_specs=[
            pl.BlockSpec((1, t, d), lambda b, i: (b, i, 0)),
            pl.BlockSpec((t, d // 2), lambda b, i: (b * steps + i, 0)),
            pl.BlockSpec((TOP_K, t), tokmap),
            pl.BlockSpec((TOP_K, t), tokmap),
            pl.BlockSpec((TOP_K, t), tokmap),
            pl.BlockSpec((N_EXPERTS, 128), lambda b, i: (0, 0)),
        ],
        out_shape=[
            jax.ShapeDtypeStruct((bsz, seq, d), F32),
            jax.ShapeDtypeStruct((n_tok, d // 2), I32),
            jax.ShapeDtypeStruct((TOP_K, n_tok), I32),
            jax.ShapeDtypeStruct((TOP_K, n_tok), F32),
            jax.ShapeDtypeStruct((TOP_K, n_tok), I32),
            jax.ShapeDtypeStruct((N_EXPERTS, 128), F32),
        ],
        scratch_shapes=[
            pltpu.VMEM((heads, HEAD_DIM, HEAD_DIM), F32),
            pltpu.VMEM((SUBLANES, conv_ch), F32),
            pltpu.VMEM((N_EXPERTS, 128), F32),
        ],
        compiler_params=pltpu.CompilerParams(
            dimension_semantics=("arbitrary", "arbitrary"), vmem_limit_bytes=VMEM_LIMIT),
        name="token_mixer",
    )(x, ada_l, n1g, win, convw, cng, lb, hng, wout, gmean, n2g, wrt, br, upper, after)


def _sc_worker():
    return lax.axis_index("s") * SC_CORES + lax.axis_index("c")


def _sc_mesh():
    return plsc.VectorSubcoreMesh(core_axis_name="c", subcore_axis_name="s")


def _sc_scatter_rows(src, pos3d, p_rows):
    kk, n_idx_rows, ch = pos3d.shape
    d = src.shape[1]
    n_ch = n_idx_rows // SC_WORKERS
    assert n_idx_rows % SC_WORKERS == 0 and n_ch % 2 == 0

    @functools.partial(
        pl.kernel, mesh=_sc_mesh(), out_type=jax.ShapeDtypeStruct((p_rows, d), src.dtype),
        scratch_types=[pltpu.VMEM((kk, n_ch, ch), I32), pltpu.VMEM((2, ch, d), src.dtype),
                       pltpu.SemaphoreType.DMA((2,)), pltpu.SemaphoreType.DMA((2,))],
        name="sc_scatter_rows")
    def scatter_kernel(src_hbm, pos_hbm, out_hbm, idx_v, rows_v, lsem, ssem):
        wid = _sc_worker()
        for j in range(kk):
            pltpu.sync_copy(pos_hbm.at[j, pl.ds(wid * n_ch, n_ch)], idx_v.at[j])
        base = wid * (n_ch * ch)

        def load(c, slot):
            return pltpu.make_async_copy(src_hbm.at[pl.ds(base + c * ch, ch)], rows_v.at[slot], lsem.at[slot])

        def scatter(c, slot, j):
            return pltpu.make_async_copy(rows_v.at[slot], out_hbm.at[idx_v.at[j, c]], ssem.at[slot])

        load(0, 0).start()

        @pl.loop(0, n_ch, step=2)
        def _(c0):
            for slot in range(2):
                c = c0 + slot
                other = 1 - slot

                @pl.when(c >= 1)
                def _():
                    for j in range(kk):
                        scatter(c - 1, other, j).wait()

                @pl.when(c + 1 < n_ch)
                def _():
                    load(c + 1, other).start()

                load(c, slot).wait()
                for j in range(kk):
                    scatter(c, slot, j).start()

        for j in range(kk):
            scatter(n_ch - 1, (n_ch - 1) % 2, j).wait()

    return scatter_kernel(src, pos3d)


def _sc_gather_rows(table, idx2d):
    n_idx_rows, ch = idx2d.shape
    d = table.shape[1]
    n_ch = n_idx_rows // SC_WORKERS
    assert n_idx_rows % SC_WORKERS == 0 and n_ch % 2 == 0

    @functools.partial(
        pl.kernel, mesh=_sc_mesh(), out_type=jax.ShapeDtypeStruct((n_idx_rows * ch, d), table.dtype),
        scratch_types=[pltpu.VMEM((n_ch, ch), I32), pltpu.VMEM((2, ch, d), table.dtype),
                       pltpu.SemaphoreType.DMA((2,)), pltpu.SemaphoreType.DMA((2,))],
        name="sc_gather_rows")
    def gather_kernel(table_hbm, idx_hbm, out_hbm, idx_v, rows_v, gsem, wsem):
        wid = _sc_worker()
        pltpu.sync_copy(idx_hbm.at[pl.ds(wid * n_ch, n_ch)], idx_v)
        base = wid * (n_ch * ch)

        def gather(c, slot):
            return pltpu.make_async_copy(table_hbm.at[idx_v.at[c]], rows_v.at[slot], gsem.at[slot])

        def write(c, slot):
            return pltpu.make_async_copy(rows_v.at[slot], out_hbm.at[pl.ds(base + c * ch, ch)], wsem.at[slot])

        gather(0, 0).start()

        @pl.loop(0, n_ch, step=2)
        def _(c0):
            for slot in range(2):
                c = c0 + slot
                other = 1 - slot

                @pl.when(c >= 1)
                def _():
                    write(c - 1, other).wait()

                @pl.when(c + 1 < n_ch)
                def _():
                    gather(c + 1, other).start()

                gather(c, slot).wait()
                write(c, slot).start()

        write(n_ch - 1, (n_ch - 1) % 2).wait()

    return gather_kernel(table, idx2d)


def _expert_kernel(eid_ref, nvalid_ref, first_ref, next_ref, slot_ref, xs_ref, wgu_hbm, bgu_ref, wdn_hbm, bdn_ref,
                   after_ref, ys_ref, wgu_st, wdn_st, wgu_bf, wdn_bf, sem_gu, sem_dn, *, layer):
    del after_ref
    i = pl.program_id(0)
    nvalid = nvalid_ref[i]
    active = nvalid > 0

    def weight_copies(expert, slot):
        return (pltpu.make_async_copy(wgu_hbm.at[layer, expert], wgu_st.at[slot], sem_gu.at[slot]),
                pltpu.make_async_copy(wdn_hbm.at[layer, expert], wdn_st.at[slot], sem_dn.at[slot]))

    @pl.when(i == 0)
    def _():
        for cp in weight_copies(eid_ref[0], slot_ref[0]):
            cp.start()

    @pl.when(first_ref[i] == 1)
    def _():
        slot = slot_ref[i]
        for cp in weight_copies(eid_ref[i], slot):
            cp.wait()

        @pl.when(next_ref[i] >= 0)
        def _():
            for cp in weight_copies(next_ref[i], 1 - slot):
                cp.start()

        def cast_rows(r, carry):
            rows = pl.ds(pl.multiple_of(r * WEIGHT_CAST_ROWS, WEIGHT_CAST_ROWS), WEIGHT_CAST_ROWS)
            wgu_bf[rows, :] = wgu_st[slot, rows, :].astype(BF16)
            wdn_bf[rows, :] = wdn_st[slot, rows, :].astype(BF16)
            return carry
        lax.fori_loop(0, wgu_bf.shape[0] // WEIGHT_CAST_ROWS, cast_rows, 0)

    def mlp_rows(n_rows):
        packed = xs_ref[0:n_rows, :]
        keep = lax.broadcasted_iota(I32, packed.shape, 0) < nvalid
        lo, hi = _unpack_bf16_pair(jnp.where(keep, packed, 0))
        dh = lo.shape[1]
        hgu = (_dot(lo.astype(BF16), wgu_bf[0:dh, :]) + _dot(hi.astype(BF16), wgu_bf[dh:2 * dh, :])
               + bgu_ref[0, 0])
        f = hgu.shape[1] // 2
        a = jnp.minimum(hgu[:, 0:f], SWIGLU_LIMIT)
        g = jnp.clip(hgu[:, f:2 * f], -SWIGLU_LIMIT, SWIGLU_LIMIT)
        act = a * _sigmoid(SWIGLU_ALPHA * a) * (g + 1.0)
        y = _dot(act.astype(BF16), wdn_bf[...]) + bdn_ref[0, 0]
        do = y.shape[1] // 2
        ys_ref[0:n_rows, :] = _pack_bf16_pair(y[:, 0:do], y[:, do:2 * do])

    full_rows = xs_ref.shape[0]
    half_rows = full_rows // 2

    @pl.when(nvalid > half_rows)
    def _():
        mlp_rows(full_rows)

    @pl.when(active & (nvalid <= half_rows))
    def _():
        mlp_rows(half_rows)
        ys_ref[half_rows:full_rows, :] = jnp.zeros((full_rows - half_rows, ys_ref.shape[1]), ys_ref.dtype)

    @pl.when(jnp.logical_not(active))
    def _():
        ys_ref[...] = jnp.zeros_like(ys_ref)


def _expert_call(layer, tables, xs, wgu, bgu, wdn, bdn, after):
    p_rows, dh = xs.shape
    bm = EXPERT_ROWS
    depth, n_e, d, f2 = wgu.shape
    dout = wdn.shape[3]
    assert f2 // 2 == d and d % WEIGHT_CAST_ROWS == 0
    bias_map = lambda i, eid, *_: (layer, eid[i], 0, 0)
    grid_spec = pltpu.PrefetchScalarGridSpec(
        num_scalar_prefetch=len(tables),
        grid=(p_rows // bm,),
        in_specs=[
            pl.BlockSpec((bm, dh), lambda i, *_: (i, 0)),
            pl.BlockSpec(memory_space=pl.ANY),
            pl.BlockSpec((1, 1, 1, f2), bias_map),
            pl.BlockSpec(memory_space=pl.ANY),
            pl.BlockSpec((1, 1, 1, dout), bias_map),
            pl.BlockSpec(memory_space=pl.ANY),
        ],
        out_specs=pl.BlockSpec((bm, dout // 2), lambda i, *_: (i, 0)),
        scratch_shapes=[
            pltpu.VMEM((2, d, f2), F32), pltpu.VMEM((2, f2 // 2, dout), F32),
            pltpu.VMEM((d, f2), BF16), pltpu.VMEM((f2 // 2, dout), BF16),
            pltpu.SemaphoreType.DMA((2,)), pltpu.SemaphoreType.DMA((2,)),
        ],
    )
    return pl.pallas_call(
        functools.partial(_expert_kernel, layer=layer),
        grid_spec=grid_spec,
        out_shape=jax.ShapeDtypeStruct((p_rows, dout // 2), I32),
        compiler_params=pltpu.CompilerParams(
            dimension_semantics=("arbitrary",), vmem_limit_bytes=VMEM_LIMIT),
        name="moe_experts",
    )(*tables, xs, wgu, bgu.reshape(depth, n_e, 1, f2), wdn, bdn.reshape(depth, n_e, 1, dout), after)


def _combine_kernel(x_ref, gate_ref, g2_ref, fg_ref, yg_ref, after_ref, o_ref, *, final_norm):
    del after_ref
    t_rows, d = x_ref.shape
    dh = d // 2
    acc_lo = jnp.zeros((t_rows, dh), F32)
    acc_hi = jnp.zeros((t_rows, dh), F32)
    for kk in range(TOP_K):
        lo, hi = _unpack_bf16_pair(yg_ref[kk])
        gk = gate_ref[:, kk:kk + 1]
        acc_lo = acc_lo + gk * lo
        acc_hi = acc_hi + gk * hi
    out = x_ref[...] + g2_ref[...] * jnp.concatenate([acc_lo, acc_hi], axis=1)
    if final_norm:
        out = _rms(out) * fg_ref[...]
    o_ref[...] = out


def _combine_call(x2d, gates_t, g2, final_g, yg, after, final_norm):
    n_tok, d = x2d.shape
    t = COMBINE_ROWS
    kern = functools.partial(_combine_kernel, final_norm=final_norm)
    return pl.pallas_call(
        kern,
        grid=(n_tok // t,),
        in_specs=[
            pl.BlockSpec((t, d), lambda i: (i, 0)),
            pl.BlockSpec((t, TOP_K), lambda i: (i, 0)),
            pl.BlockSpec((1, d), lambda i: (0, 0)),
            pl.BlockSpec((1, d), lambda i: (0, 0)),
            pl.BlockSpec((TOP_K, t, d // 2), lambda i: (0, i, 0)),
            pl.BlockSpec(memory_space=pl.ANY),
        ],
        out_specs=pl.BlockSpec((t, d), lambda i: (i, 0)),
        out_shape=jax.ShapeDtypeStruct((n_tok, d), F32),
        compiler_params=pltpu.CompilerParams(
            dimension_semantics=("arbitrary",), vmem_limit_bytes=VMEM_LIMIT),
        name="moe_combine",
    )(x2d, gates_t, g2, final_g, yg, after)


def _route_tables(eidx, rank, cnt, n_blocks):
    bm = EXPERT_ROWS
    counts = cnt[:, 0].astype(I32)
    padded = ((counts + bm - 1) // bm) * bm
    pends = jnp.cumsum(padded)
    pstarts = pends - padded
    pos = rank
    for e in range(N_EXPERTS):
        pos = pos + jnp.where(eidx == e, pstarts[e], 0)
    block_row = jnp.arange(n_blocks, dtype=I32) * bm
    block_eid = jnp.minimum(jnp.sum((pends[None, :] <= block_row[:, None]).astype(I32), axis=1), N_EXPERTS - 1)
    experts = jnp.arange(N_EXPERTS, dtype=I32)
    onehot = block_eid[:, None] == experts[None, :]
    row_end = jnp.sum(jnp.where(onehot, (pstarts + counts)[None, :], 0), axis=1)
    block_valid = jnp.clip(row_end - block_row, 0, bm).astype(I32)
    active = block_valid > 0
    prev_eid = jnp.concatenate([jnp.full((1,), -1, I32), block_eid[:-1]])
    first = (active & (block_eid != prev_eid)).astype(I32)
    later = (experts[None, :] > experts[:, None]) & (counts[None, :] > 0)
    next_expert = jnp.min(jnp.where(later, experts[None, :], N_EXPERTS), axis=1)
    next_expert = jnp.where(next_expert == N_EXPERTS, -1, next_expert)
    block_next = jnp.sum(jnp.where(onehot, next_expert[None, :], 0), axis=1).astype(I32)
    slot = ((jnp.cumsum(first) - 1) % 2).astype(I32)
    return pos, (block_eid, block_valid, first, block_next, slot)


def _moe_rows(layer, hn2p, eidx, rank, cnt, wgu, bgu, wdn, bdn, after):
    n_tok, dh = hn2p.shape
    n_blocks = n_tok * TOP_K // EXPERT_ROWS + N_EXPERTS
    pos, tables = _route_tables(eidx, rank, cnt, n_blocks)
    xs = _sc_scatter_rows(hn2p, pos.reshape(TOP_K, n_tok // SC_CHUNK, SC_CHUNK), n_blocks * EXPERT_ROWS)
    ys = _expert_call(layer, tables, xs, wgu, bgu, wdn, bdn, after)
    yg = _sc_gather_rows(ys, pos.reshape(TOP_K * n_tok // SC_CHUNK, SC_CHUNK))
    return ys, yg.reshape(TOP_K, n_tok, dh)


def kernel(x, c, w_ada, b_ada, norm1_g, norm2_g, w_in, conv_w, conv_norm_g, lower_bounds, hgrn_norm_g, w_out,
           w_router, b_router, w_gu, b_gu, w_down, b_down, final_g):
    depth = w_ada.shape[0]
    bsz, seq, d = x.shape
    conv_ch = conv_w.shape[2]
    c_pad = jnp.zeros((SUBLANES, d), F32).at[:bsz].set(c)
    ada = _ada_call(c_pad, w_ada, b_ada)[:, :bsz].reshape(depth, bsz, 6, d)
    lb_all = _bounds_call(lower_bounds)

    ci = jnp.arange(conv_ch)
    gmean = jnp.where((ci[:, None] // CONV_GROUP) == (ci[None, :] // CONV_GROUP), 1.0 / CONV_GROUP, 0.0).astype(BF16)
    ti = jnp.arange(MIX_ROWS)
    upper = (ti[:, None] < ti[None, :]).astype(BF16)

    def mixer(l, b, xrow, after):
        win, wout = w_in[l].astype(BF16), w_out[l].astype(BF16)
        wr_hi = w_router[l].astype(BF16)
        wr_lo = (w_router[l] - wr_hi.astype(F32)).astype(BF16)
        wrt = jnp.zeros((d, 256), BF16).at[:, :N_EXPERTS].set(wr_hi).at[:, 128:128 + N_EXPERTS].set(wr_lo)
        br = jnp.broadcast_to(b_router[l][:, None], (N_EXPERTS, 128))
        return _mixer_call(xrow, ada[l, b:b + 1], norm1_g[l][None], win, conv_w[l], conv_norm_g[l][None],
                           lb_all[l][None], hgrn_norm_g[l][None], wout, gmean, norm2_g[l][None], wrt, br, upper,
                           after)

    def combine(l, b, mixed, yg, after):
        xn, _, _, gates, _, _ = mixed
        return _combine_call(xn.reshape(seq, d), gates.T, ada[l, b, 5:6, :], final_g[None], yg, after,
                             final_norm=(l == depth - 1)).reshape(1, seq, d)

    assert bsz == 2
    x0, x1 = x[0:1], x[1:2]
    pending = None
    after = c_pad
    for l in range(depth):
        m0 = mixer(l, 0, x0, after)
        if pending is not None:
            x1 = combine(l - 1, 1, *pending, m0[5])
        m1 = mixer(l, 1, x1, m0[5])
        ys0, yg0 = _moe_rows(l, m0[1], m0[2], m0[4], m0[5], w_gu, b_gu, w_down, b_down, m1[5])
        ys1, yg1 = _moe_rows(l, m1[1], m1[2], m1[4], m1[5], w_gu, b_gu, w_down, b_down, ys0)
        x0 = combine(l, 0, m0, yg0, ys1)
        pending = (m1, yg1)
        after = x0
    x1 = combine(depth - 1, 1, *pending, x0)
    return jnp.concatenate([x0, x1], axis=0)
```

```python
import functools

import jax
import jax.numpy as jnp
from jax import lax
from jax.experimental import pallas as pl
from jax.experimental.pallas import tpu as pltpu
from jax.experimental.pallas import tpu_sc as plsc

F32 = jnp.float32
BF16 = jnp.bfloat16
I32 = jnp.int32
HIGHEST = lax.Precision.HIGHEST

RMS_EPS = 1e-6
N_EXPERTS = 32
TOP_K = 4
CONV_GROUP = 64
HEAD_DIM = 128
SWIGLU_LIMIT = 7.0
SWIGLU_ALPHA = 1.702
SUBLANES = 8
VMEM_LIMIT = 56 * 1024 * 1024

MIX_ROWS = 256
EXPERT_ROWS = 512
COMBINE_ROWS = 512
SC_CORES = 2
SC_SUBCORES = 16
SC_WORKERS = SC_CORES * SC_SUBCORES
SC_CHUNK = 64
ADA_COLS = 1536
WEIGHT_CAST_ROWS = 128


def _dot(a, b):
    return jnp.dot(a, b, preferred_element_type=F32)


def _dot_nt(a, b, precision=None):
    return lax.dot_general(a, b, (((1,), (1,)), ((), ())), precision=precision,
                           preferred_element_type=F32)


def _dot_tn(a, b):
    return lax.dot_general(a, b, (((0,), (0,)), ((), ())), preferred_element_type=F32)


def _sigmoid(x):
    return 1.0 / (1.0 + jnp.exp(-x))


def _rms(x):
    return x * lax.rsqrt(jnp.mean(x * x, axis=-1, keepdims=True) + RMS_EPS)


def _pack_bf16_pair(lo, hi):
    lo_bits = lax.bitcast_convert_type(lo.astype(BF16).astype(F32), I32)
    hi_bits = lax.bitcast_convert_type(hi.astype(BF16).astype(F32), I32)
    return lax.shift_right_logical(lo_bits, 16) | (hi_bits & jnp.int32(-65536))


def _unpack_bf16_pair(p):
    lo = lax.bitcast_convert_type(lax.shift_left(p, 16), F32)
    hi = lax.bitcast_convert_type(p & jnp.int32(-65536), F32)
    return lo, hi


def _ada_kernel(c_ref, w_ref, b_ref, o_ref):
    cv = c_ref[...]
    ca = cv * _sigmoid(cv)
    o_ref[0] = jnp.dot(ca, w_ref[0], precision=HIGHEST, preferred_element_type=F32) + b_ref[0]


def _ada_call(c_pad, w_ada, b_ada):
    depth, d, n6 = w_ada.shape
    rows = c_pad.shape[0]
    return pl.pallas_call(
        _ada_kernel,
        grid=(depth, n6 // ADA_COLS),
        in_specs=[
            pl.BlockSpec((rows, d), lambda l, j: (0, 0)),
            pl.BlockSpec((1, d, ADA_COLS), lambda l, j: (l, 0, j)),
            pl.BlockSpec((1, 1, ADA_COLS), lambda l, j: (l, 0, j)),
        ],
        out_specs=pl.BlockSpec((1, rows, ADA_COLS), lambda l, j: (l, 0, j)),
        out_shape=jax.ShapeDtypeStruct((depth, rows, n6), F32),
        compiler_params=pltpu.CompilerParams(
            dimension_semantics=("arbitrary", "arbitrary"), vmem_limit_bytes=VMEM_LIMIT),
        name="ada_proj",
    )(c_pad, w_ada, b_ada.reshape(depth, 1, n6))


def _bounds_kernel(lb_ref, o_ref):
    rows = [lb_ref[l:l + 1, :] for l in range(lb_ref.shape[0])]
    m = functools.reduce(jnp.maximum, rows)
    es = [jnp.exp(r - m) for r in rows]
    tot = functools.reduce(lambda a, b: a + b, es)
    cum = None
    first = None
    for l, e in enumerate(es):
        p = e / tot
        cum = p if cum is None else cum + p
        if first is None:
            first = cum
        o_ref[l:l + 1, :] = cum - first


def _bounds_call(lower_bounds):
    return pl.pallas_call(
        _bounds_kernel,
        out_shape=jax.ShapeDtypeStruct(lower_bounds.shape, F32),
        name="hgrn_bounds",
    )(lower_bounds.astype(F32))


def _small_level_exponents(g, row):
    t_rows, width = g.shape
    blocks = t_rows // SUBLANES

    def blockroll(x, k):
        return pltpu.roll(x.reshape(blocks, SUBLANES, width), k, 1).reshape(t_rows, width)

    a1 = g + blockroll(g, 1)
    a2 = a1 + blockroll(g, 2)
    a3 = a2 + blockroll(g, 3)
    b1 = blockroll(g, SUBLANES - 1)
    b2 = b1 + blockroll(g, SUBLANES - 2)
    b3 = b2 + blockroll(g, SUBLANES - 3)
    o = row & 7
    o2 = row & 3
    arg1 = jnp.where((row & 1) != 0, g, 0.0)
    arg2 = jnp.where(o2 == 3, a1, jnp.where(o2 == 2, g, jnp.where(o2 == 0, b1, 0.0)))
    up4 = jnp.where(o == 7, a3, jnp.where(o == 6, a2, jnp.where(o == 5, a1, g)))
    lo4 = jnp.where(o == 0, b3, jnp.where(o == 1, b2, jnp.where(o == 2, b1, 0.0)))
    arg4 = jnp.where(o >= 4, up4, lo4)
    return {1: arg1, 2: arg2, 4: arg4}


def _hgrn_head(q, k, v, g_cum, small_args, group_masks, st_ref, h, row):
    t_rows = q.shape[0]
    scores = jnp.zeros((t_rows, t_rows), F32)
    half = t_rows // 2
    while half >= 1:
        grp = 2 * half
        upper = (row & half) != 0
        if half >= SUBLANES:
            g_mid = g_cum.reshape(t_rows // grp, grp, HEAD_DIM)[:, half - 1:half, :]
            g_mid = jnp.broadcast_to(g_mid, (t_rows // grp, grp, HEAD_DIM)).reshape(t_rows, HEAD_DIM)
            d = g_cum - g_mid
            arg = jnp.where(upper, d, -d)
        else:
            arg = small_args[half]
        e = jnp.exp(arg)
        qh = jnp.where(upper, q * e, 0.0).astype(BF16)
        kh = jnp.where(upper, 0.0, k * e).astype(BF16)
        sc = _dot_nt(qh, kh)
        if grp < t_rows:
            sc = jnp.where(group_masks[grp], sc, 0.0)
        scores = scores + sc
        half //= 2

    out = _dot(scores.astype(BF16), v.astype(BF16)) + jnp.sum(q * k, axis=-1, keepdims=True) * v

    st = st_ref[h]
    out = out + _dot_nt((q * jnp.exp(g_cum)).astype(BF16), st.astype(BF16))
    g_last = g_cum[t_rows - 1:t_rows, :]
    kd = (k * jnp.exp(g_last - g_cum)).astype(BF16)
    st_ref[h] = st * jnp.exp(g_last) + _dot_tn(v.astype(BF16), kd)
    return out


def _moe_residual(x, gate_cols, g2, yg_ref):
    t_rows, d = x.shape
    acc_lo = jnp.zeros((t_rows, d // 2), F32)
    acc_hi = jnp.zeros((t_rows, d // 2), F32)
    for kk in range(TOP_K):
        lo, hi = _unpack_bf16_pair(yg_ref[kk])
        gk = gate_cols[:, kk:kk + 1]
        acc_lo = acc_lo + gk * lo
        acc_hi = acc_hi + gk * hi
    return x + g2 * jnp.concatenate([acc_lo, acc_hi], axis=1)


def _mixer_kernel(*refs, conv_ch, hgrn_w, with_moe_input):
    if with_moe_input:
        pgate_ref, pg2_ref, yg_ref = refs[0:3]
        refs = refs[3:]
    (x_ref, ada_ref, n1g_ref, win_ref, convw_ref, cng_ref, lb_ref, hng_ref, wout_ref,
     gmean_ref, n2g_ref, wrt_ref, br_ref, upper_ref, after_ref,
     xo_ref, hn2_ref, eidx_ref, gate_ref, rank_ref, cnt_ref,
     st_ref, carry_ref, base_ref) = refs
    del after_ref
    b = pl.program_id(0)
    i = pl.program_id(1)
    t_rows = x_ref.shape[1]
    heads = hgrn_w // HEAD_DIM

    @pl.when(i == 0)
    def _():
        st_ref[...] = jnp.zeros_like(st_ref)
        carry_ref[...] = jnp.zeros_like(carry_ref)

    @pl.when((b == 0) & (i == 0))
    def _():
        base_ref[...] = jnp.zeros_like(base_ref)

    sh1, sc1, g1 = ada_ref[0, 0:1, :], ada_ref[0, 1:2, :], ada_ref[0, 2:3, :]
    sh2, sc2, g2 = ada_ref[0, 3:4, :], ada_ref[0, 4:5, :], ada_ref[0, 5:6, :]
    del g2

    x = x_ref[0]
    if with_moe_input:
        x = _moe_residual(x, pgate_ref[...], pg2_ref[0], yg_ref)
    hn = (_rms(x) * n1g_ref[...]) * (1.0 + sc1) + sh1
    proj = _dot(hn.astype(BF16), win_ref[...])
    c0 = conv_ch
    cb, cc, ch = proj[:, 0:c0], proj[:, c0:2 * c0], proj[:, 2 * c0:3 * c0]
    o0 = 3 * c0
    q_raw = proj[:, o0:o0 + hgrn_w]
    f_raw = proj[:, o0 + hgrn_w:o0 + 2 * hgrn_w]
    v_all = proj[:, o0 + 2 * hgrn_w:o0 + 3 * hgrn_w]
    og = proj[:, o0 + 3 * hgrn_w:o0 + 4 * hgrn_w]

    u = cc * ch
    carry = carry_ref[...]
    row8 = lax.broadcasted_iota(I32, (SUBLANES, c0), 0)
    u1 = pltpu.roll(u, 1, 0)
    u2 = pltpu.roll(u, 2, 0)
    u1 = jnp.concatenate([jnp.where(row8 < 1, pltpu.roll(carry, 1, 0), u1[0:SUBLANES]), u1[SUBLANES:]], axis=0)
    u2 = jnp.concatenate([jnp.where(row8 < 2, pltpu.roll(carry, 2, 0), u2[0:SUBLANES]), u2[SUBLANES:]], axis=0)
    carry_ref[...] = u[t_rows - SUBLANES:t_rows]
    yc = cb * (convw_ref[0:1, :] * u2 + convw_ref[1:2, :] * u1 + convw_ref[2:3, :] * u)
    sq = yc * yc
    sq_hi = sq.astype(BF16)
    sq_lo = (sq - sq_hi.astype(F32)).astype(BF16)
    gms = _dot(sq_hi, gmean_ref[...]) + _dot(sq_lo, gmean_ref[...])
    ycn = yc * lax.rsqrt(gms + RMS_EPS) * cng_ref[...]

    lb = lb_ref[...]
    log_lb = jnp.log(lb)
    log_1mlb = jnp.log1p(-lb)
    log_sig = jnp.minimum(f_raw, 0.0) - jnp.log(1.0 + jnp.exp(-jnp.abs(f_raw)))
    bb = log_1mlb + log_sig
    log_f = jnp.maximum(log_lb, bb) + jnp.log(1.0 + jnp.exp(-jnp.abs(log_lb - bb)))
    k_all = 1.0 - jnp.exp(log_f)
    q_all = q_raw * _sigmoid(q_raw)
    roww = lax.broadcasted_iota(I32, (t_rows, hgrn_w), 0)
    g_cum = log_f
    s = 1
    while s < t_rows:
        g_cum = g_cum + jnp.where(roww >= s, pltpu.roll(g_cum, s, 0), 0.0)
        s *= 2
    small = _small_level_exponents(log_f, roww)
    row = lax.broadcasted_iota(I32, (t_rows, HEAD_DIM), 0)
    rowsq = lax.broadcasted_iota(I32, (t_rows, t_rows), 0)
    col = lax.broadcasted_iota(I32, (t_rows, t_rows), 1)
    group_masks = {}
    grp = 2
    while grp < t_rows:
        shift = grp.bit_length() - 1
        group_masks[grp] = lax.shift_right_logical(rowsq, shift) == lax.shift_right_logical(col, shift)
        grp *= 2
    outs = []
    for h in range(heads):
        sl = slice(h * HEAD_DIM, (h + 1) * HEAD_DIM)
        small_h = {lvl: a[:, sl] for lvl, a in small.items()}
        o = _hgrn_head(q_all[:, sl], k_all[:, sl], v_all[:, sl], g_cum[:, sl], small_h, group_masks, st_ref, h,
                       row)
        outs.append(_rms(o))
    oh = jnp.concatenate(outs, axis=1) * hng_ref[...] * (og * _sigmoid(og))

    mix = _dot(ycn.astype(BF16), wout_ref[0:c0, :]) + _dot(oh.astype(BF16), wout_ref[c0:c0 + hgrn_w, :])
    xn = x + g1 * mix
    xo_ref[0] = xn

    hn2 = (_rms(xn) * n2g_ref[...]) * (1.0 + sc2) + sh2
    dh = hn2.shape[1] // 2
    hn2_ref[...] = _pack_bf16_pair(hn2[:, 0:dh], hn2[:, dh:2 * dh])
    h_hi = hn2.astype(BF16)
    h_lo = (hn2 - h_hi.astype(F32)).astype(BF16)
    prod = _dot(h_hi, wrt_ref[...])
    logits_tok = prod[:, 0:128] + prod[:, 128:256] + _dot(h_lo, wrt_ref[:, 0:128])
    logits = logits_tok.T[0:N_EXPERTS, :] + br_ref[:, 0:1]
    eio = lax.broadcasted_iota(I32, logits.shape, 0)
    order = jnp.zeros(logits.shape, F32)
    for e2 in range(N_EXPERTS):
        other = logits[e2:e2 + 1, :]
        beats = (other > logits) | ((other == logits) & (eio > e2))
        order = order + beats.astype(F32)
    eio_f = eio.astype(F32)
    vals, sels = [], []
    for kk in range(TOP_K):
        sel = order == float(kk)
        vals.append(jnp.sum(jnp.where(sel, logits, 0.0), axis=0, keepdims=True))
        sels.append(sel)
        eidx_ref[kk:kk + 1, :] = jnp.sum(jnp.where(sel, eio_f, 0.0), axis=0, keepdims=True).astype(I32)
    exps = [jnp.exp(vv - vals[0]) for vv in vals]
    tot = functools.reduce(lambda a, c: a + c, exps)
    member = functools.reduce(lambda a, c: a + c, [s_.astype(F32) for s_ in sels])
    before = _dot(member.astype(BF16), upper_ref[...])
    base = base_ref[:, 0:1]
    slot = before + base
    for kk in range(TOP_K):
        gate_ref[kk:kk + 1, :] = exps[kk] / tot
        rank_ref[kk:kk + 1, :] = jnp.sum(jnp.where(sels[kk], slot, 0.0), axis=0, keepdims=True).astype(I32)
    base_new = base + jnp.sum(member, axis=1, keepdims=True)
    base_ref[...] = jnp.broadcast_to(base_new, base_ref.shape)
    cnt_ref[...] = jnp.broadcast_to(base_new, cnt_ref.shape)


def _mixer_call(x, row, ada_l, n1g, win, convw, cng, lb, hng, wout, gmean, n2g, wrt, br, upper, after,
                moe_input=None):
    _, seq, d = x.shape
    bsz = 1
    t = MIX_ROWS
    n_tok = bsz * seq
    conv_ch = convw.shape[1]
    hgrn_w = lb.shape[1]
    heads = hgrn_w // HEAD_DIM
    steps = seq // t
    full = lambda a: pl.BlockSpec(a.shape, lambda b, i: (0,) * a.ndim)
    tokmap = lambda b, i: (0, b * steps + i)
    kern = functools.partial(_mixer_kernel, conv_ch=conv_ch, hgrn_w=hgrn_w, with_moe_input=moe_input is not None)
    moe_specs, moe_args = [], ()
    if moe_input is not None:
        moe_specs = [
            pl.BlockSpec((t, TOP_K), lambda b, i: (i, 0)),
            pl.BlockSpec((1, 1, d), lambda b, i: (0, 0, 0)),
            pl.BlockSpec((TOP_K, t, d // 2), lambda b, i: (0, i, 0)),
        ]
        moe_args = tuple(moe_input)
    return pl.pallas_call(
        kern,
        grid=(bsz, steps),
        in_specs=moe_specs + [
            pl.BlockSpec((1, t, d), lambda b, i: (row, i, 0)),
            pl.BlockSpec((1, 6, d), lambda b, i: (b, 0, 0)),
            full(n1g), full(win), full(convw), full(cng), full(lb), full(hng), full(wout),
            full(gmean), full(n2g), full(wrt), full(br), full(upper),
            pl.BlockSpec(memory_space=pl.ANY),
        ],
        out_specs=[
            pl.BlockSpec((1, t, d), lambda b, i: (b, i, 0)),
            pl.BlockSpec((t, d // 2), lambda b, i: (b * steps + i, 0)),
            pl.BlockSpec((TOP_K, t), tokmap),
            pl.BlockSpec((TOP_K, t), tokmap),
            pl.BlockSpec((TOP_K, t), tokmap),
            pl.BlockSpec((N_EXPERTS, 128), lambda b, i: (0, 0)),
        ],
        out_shape=[
            jax.ShapeDtypeStruct((bsz, seq, d), F32),
            jax.ShapeDtypeStruct((n_tok, d // 2), I32),
            jax.ShapeDtypeStruct((TOP_K, n_tok), I32),
            jax.ShapeDtypeStruct((TOP_K, n_tok), F32),
            jax.ShapeDtypeStruct((TOP_K, n_tok), I32),
            jax.ShapeDtypeStruct((N_EXPERTS, 128), F32),
        ],
        scratch_shapes=[
            pltpu.VMEM((heads, HEAD_DIM, HEAD_DIM), F32),
            pltpu.VMEM((SUBLANES, conv_ch), F32),
            pltpu.VMEM((N_EXPERTS, 128), F32),
        ],
        compiler_params=pltpu.CompilerParams(
            dimension_semantics=("arbitrary", "arbitrary"), vmem_limit_bytes=VMEM_LIMIT),
        name="token_mixer",
    )(*moe_args, x, ada_l, n1g, win, convw, cng, lb, hng, wout, gmean, n2g, wrt, br, upper, after)


def _sc_worker():
    return lax.axis_index("s") * SC_CORES + lax.axis_index("c")


def _sc_mesh():
    return plsc.VectorSubcoreMesh(core_axis_name="c", subcore_axis_name="s")


def _sc_scatter_rows(src, pos3d, p_rows):
    kk, n_idx_rows, ch = pos3d.shape
    d = src.shape[1]
    n_ch = n_idx_rows // SC_WORKERS
    assert n_idx_rows % SC_WORKERS == 0 and n_ch % 2 == 0

    @functools.partial(
        pl.kernel, mesh=_sc_mesh(), out_type=jax.ShapeDtypeStruct((p_rows, d), src.dtype),
        scratch_types=[pltpu.VMEM((kk, n_ch, ch), I32), pltpu.VMEM((2, ch, d), src.dtype),
                       pltpu.SemaphoreType.DMA((2,)), pltpu.SemaphoreType.DMA((2,))],
        name="sc_scatter_rows")
    def scatter_kernel(src_hbm, pos_hbm, out_hbm, idx_v, rows_v, lsem, ssem):
        wid = _sc_worker()
        for j in range(kk):
            pltpu.sync_copy(pos_hbm.at[j, pl.ds(wid * n_ch, n_ch)], idx_v.at[j])
        base = wid * (n_ch * ch)

        def load(c, slot):
            return pltpu.make_async_copy(src_hbm.at[pl.ds(base + c * ch, ch)], rows_v.at[slot], lsem.at[slot])

        def scatter(c, slot, j):
            return pltpu.make_async_copy(rows_v.at[slot], out_hbm.at[idx_v.at[j, c]], ssem.at[slot])

        load(0, 0).start()

        @pl.loop(0, n_ch, step=2)
        def _(c0):
            for slot in range(2):
                c = c0 + slot
                other = 1 - slot

                @pl.when(c >= 1)
                def _():
                    for j in range(kk):
                        scatter(c - 1, other, j).wait()

                @pl.when(c + 1 < n_ch)
                def _():
                    load(c + 1, other).start()

                load(c, slot).wait()
                for j in range(kk):
                    scatter(c, slot, j).start()

        for j in range(kk):
            scatter(n_ch - 1, (n_ch - 1) % 2, j).wait()

    return scatter_kernel(src, pos3d)


def _sc_gather_rows(table, idx2d):
    n_idx_rows, ch = idx2d.shape
    d = table.shape[1]
    n_ch = n_idx_rows // SC_WORKERS
    assert n_idx_rows % SC_WORKERS == 0 and n_ch % 2 == 0

    @functools.partial(
        pl.kernel, mesh=_sc_mesh(), out_type=jax.ShapeDtypeStruct((n_idx_rows * ch, d), table.dtype),
        scratch_types=[pltpu.VMEM((n_ch, ch), I32), pltpu.VMEM((2, ch, d), table.dtype),
                       pltpu.SemaphoreType.DMA((2,)), pltpu.SemaphoreType.DMA((2,))],
        name="sc_gather_rows")
    def gather_kernel(table_hbm, idx_hbm, out_hbm, idx_v, rows_v, gsem, wsem):
        wid = _sc_worker()
        pltpu.sync_copy(idx_hbm.at[pl.ds(wid * n_ch, n_ch)], idx_v)
        base = wid * (n_ch * ch)

        def gather(c, slot):
            return pltpu.make_async_copy(table_hbm.at[idx_v.at[c]], rows_v.at[slot], gsem.at[slot])

        def write(c, slot):
            return pltpu.make_async_copy(rows_v.at[slot], out_hbm.at[pl.ds(base + c * ch, ch)], wsem.at[slot])

        gather(0, 0).start()

        @pl.loop(0, n_ch, step=2)
        def _(c0):
            for slot in range(2):
                c = c0 + slot
                other = 1 - slot

                @pl.when(c >= 1)
                def _():
                    write(c - 1, other).wait()

                @pl.when(c + 1 < n_ch)
                def _():
                    gather(c + 1, other).start()

                gather(c, slot).wait()
                write(c, slot).start()

        write(n_ch - 1, (n_ch - 1) % 2).wait()

    return gather_kernel(table, idx2d)


def _expert_kernel(eid_ref, nvalid_ref, first_ref, next_ref, slot_ref, xs_ref, wgu_hbm, bgu_ref, wdn_hbm, bdn_ref,
                   after_ref, ys_ref, wgu_st, wdn_st, wgu_bf, wdn_bf, sem_gu, sem_dn, *, layer):
    del after_ref
    i = pl.program_id(0)
    nvalid = nvalid_ref[i]
    active = nvalid > 0

    def weight_copies(expert, slot):
        return (pltpu.make_async_copy(wgu_hbm.at[layer, expert], wgu_st.at[slot], sem_gu.at[slot]),
                pltpu.make_async_copy(wdn_hbm.at[layer, expert], wdn_st.at[slot], sem_dn.at[slot]))

    @pl.when(i == 0)
    def _():
        for cp in weight_copies(eid_ref[0], slot_ref[0]):
            cp.start()

    @pl.when(first_ref[i] == 1)
    def _():
        slot = slot_ref[i]
        for cp in weight_copies(eid_ref[i], slot):
            cp.wait()

        @pl.when(next_ref[i] >= 0)
        def _():
            for cp in weight_copies(next_ref[i], 1 - slot):
                cp.start()

        def cast_rows(r, carry):
            rows = pl.ds(pl.multiple_of(r * WEIGHT_CAST_ROWS, WEIGHT_CAST_ROWS), WEIGHT_CAST_ROWS)
            wgu_bf[rows, :] = wgu_st[slot, rows, :].astype(BF16)
            wdn_bf[rows, :] = wdn_st[slot, rows, :].astype(BF16)
            return carry
        lax.fori_loop(0, wgu_bf.shape[0] // WEIGHT_CAST_ROWS, cast_rows, 0)

    def mlp_rows(n_rows):
        packed = xs_ref[0:n_rows, :]
        keep = lax.broadcasted_iota(I32, packed.shape, 0) < nvalid
        lo, hi = _unpack_bf16_pair(jnp.where(keep, packed, 0))
        dh = lo.shape[1]
        hgu = (_dot(lo.astype(BF16), wgu_bf[0:dh, :]) + _dot(hi.astype(BF16), wgu_bf[dh:2 * dh, :])
               + bgu_ref[0, 0])
        f = hgu.shape[1] // 2
        a = jnp.minimum(hgu[:, 0:f], SWIGLU_LIMIT)
        g = jnp.clip(hgu[:, f:2 * f], -SWIGLU_LIMIT, SWIGLU_LIMIT)
        act = a * _sigmoid(SWIGLU_ALPHA * a) * (g + 1.0)
        y = _dot(act.astype(BF16), wdn_bf[...]) + bdn_ref[0, 0]
        do = y.shape[1] // 2
        ys_ref[0:n_rows, :] = _pack_bf16_pair(y[:, 0:do], y[:, do:2 * do])

    full_rows = xs_ref.shape[0]
    half_rows = full_rows // 2

    @pl.when(nvalid > half_rows)
    def _():
        mlp_rows(full_rows)

    @pl.when(active & (nvalid <= half_rows))
    def _():
        mlp_rows(half_rows)
        ys_ref[half_rows:full_rows, :] = jnp.zeros((full_rows - half_rows, ys_ref.shape[1]), ys_ref.dtype)

    @pl.when(jnp.logical_not(active))
    def _():
        ys_ref[...] = jnp.zeros_like(ys_ref)


def _expert_call(layer, tables, xs, wgu, bgu, wdn, bdn, after):
    p_rows, dh = xs.shape
    bm = EXPERT_ROWS
    depth, n_e, d, f2 = wgu.shape
    dout = wdn.shape[3]
    assert f2 // 2 == d and d % WEIGHT_CAST_ROWS == 0
    bias_map = lambda i, eid, *_: (layer, eid[i], 0, 0)
    grid_spec = pltpu.PrefetchScalarGridSpec(
        num_scalar_prefetch=len(tables),
        grid=(p_rows // bm,),
        in_specs=[
            pl.BlockSpec((bm, dh), lambda i, *_: (i, 0)),
            pl.BlockSpec(memory_space=pl.ANY),
            pl.BlockSpec((1, 1, 1, f2), bias_map),
            pl.BlockSpec(memory_space=pl.ANY),
            pl.BlockSpec((1, 1, 1, dout), bias_map),
            pl.BlockSpec(memory_space=pl.ANY),
        ],
        out_specs=pl.BlockSpec((bm, dout // 2), lambda i, *_: (i, 0)),
        scratch_shapes=[
            pltpu.VMEM((2, d, f2), F32), pltpu.VMEM((2, f2 // 2, dout), F32),
            pltpu.VMEM((d, f2), BF16), pltpu.VMEM((f2 // 2, dout), BF16),
            pltpu.SemaphoreType.DMA((2,)), pltpu.SemaphoreType.DMA((2,)),
        ],
    )
    return pl.pallas_call(
        functools.partial(_expert_kernel, layer=layer),
        grid_spec=grid_spec,
        out_shape=jax.ShapeDtypeStruct((p_rows, dout // 2), I32),
        compiler_params=pltpu.CompilerParams(
            dimension_semantics=("arbitrary",), vmem_limit_bytes=VMEM_LIMIT),
        name="moe_experts",
    )(*tables, xs, wgu, bgu.reshape(depth, n_e, 1, f2), wdn, bdn.reshape(depth, n_e, 1, dout), after)


def _combine_kernel(x_ref, gate_ref, g2_ref, fg_ref, yg_ref, after_ref, o_ref, *, final_norm):
    del after_ref
    out = _moe_residual(x_ref[...], gate_ref[...], g2_ref[...], yg_ref)
    if final_norm:
        out = _rms(out) * fg_ref[...]
    o_ref[...] = out


def _combine_call(x2d, gates_t, g2, final_g, yg, after, row, n_rows, final_norm):
    n_tok, d = x2d.shape
    t = COMBINE_ROWS
    kern = functools.partial(_combine_kernel, final_norm=final_norm)
    return pl.pallas_call(
        kern,
        grid=(n_tok // t,),
        in_specs=[
            pl.BlockSpec((t, d), lambda i: (i, 0)),
            pl.BlockSpec((t, TOP_K), lambda i: (i, 0)),
            pl.BlockSpec((1, d), lambda i: (0, 0)),
            pl.BlockSpec((1, d), lambda i: (0, 0)),
            pl.BlockSpec((TOP_K, t, d // 2), lambda i: (0, i, 0)),
            pl.BlockSpec(memory_space=pl.ANY),
        ],
        out_specs=pl.BlockSpec((None, t, d), lambda i: (row, i, 0)),
        out_shape=jax.ShapeDtypeStruct((n_rows, n_tok, d), F32),
        input_output_aliases={5: 0} if row > 0 else {},
        compiler_params=pltpu.CompilerParams(
            dimension_semantics=("arbitrary",), vmem_limit_bytes=VMEM_LIMIT),
        name="moe_combine",
    )(x2d, gates_t, g2, final_g, yg, after)


def _route_tables(eidx, rank, cnt, n_blocks):
    bm = EXPERT_ROWS
    counts = cnt[:, 0].astype(I32)
    padded = ((counts + bm - 1) // bm) * bm
    pends = jnp.cumsum(padded)
    pstarts = pends - padded
    pos = rank
    for e in range(N_EXPERTS):
        pos = pos + jnp.where(eidx == e, pstarts[e], 0)
    block_row = jnp.arange(n_blocks, dtype=I32) * bm
    block_eid = jnp.minimum(jnp.sum((pends[None, :] <= block_row[:, None]).astype(I32), axis=1), N_EXPERTS - 1)
    experts = jnp.arange(N_EXPERTS, dtype=I32)
    onehot = block_eid[:, None] == experts[None, :]
    row_end = jnp.sum(jnp.where(onehot, (pstarts + counts)[None, :], 0), axis=1)
    block_valid = jnp.clip(row_end - block_row, 0, bm).astype(I32)
    active = block_valid > 0
    prev_eid = jnp.concatenate([jnp.full((1,), -1, I32), block_eid[:-1]])
    first = (active & (block_eid != prev_eid)).astype(I32)
    later = (experts[None, :] > experts[:, None]) & (counts[None, :] > 0)
    next_expert = jnp.min(jnp.where(later, experts[None, :], N_EXPERTS), axis=1)
    next_expert = jnp.where(next_expert == N_EXPERTS, -1, next_expert)
    block_next = jnp.sum(jnp.where(onehot, next_expert[None, :], 0), axis=1).astype(I32)
    slot = ((jnp.cumsum(first) - 1) % 2).astype(I32)
    return pos, (block_eid, block_valid, first, block_next, slot)


def _moe_rows(layer, hn2p, eidx, rank, cnt, wgu, bgu, wdn, bdn, after):
    n_tok, dh = hn2p.shape
    n_blocks = n_tok * TOP_K // EXPERT_ROWS + N_EXPERTS
    pos, tables = _route_tables(eidx, rank, cnt, n_blocks)
    xs = _sc_scatter_rows(hn2p, pos.reshape(TOP_K, n_tok // SC_CHUNK, SC_CHUNK), n_blocks * EXPERT_ROWS)
    ys = _expert_call(layer, tables, xs, wgu, bgu, wdn, bdn, after)
    yg = _sc_gather_rows(ys, pos.reshape(TOP_K * n_tok // SC_CHUNK, SC_CHUNK))
    return ys, yg.reshape(TOP_K, n_tok, dh)


def kernel(x, c, w_ada, b_ada, norm1_g, norm2_g, w_in, conv_w, conv_norm_g, lower_bounds, hgrn_norm_g, w_out,
           w_router, b_router, w_gu, b_gu, w_down, b_down, final_g):
    depth = w_ada.shape[0]
    bsz, seq, d = x.shape
    conv_ch = conv_w.shape[2]
    c_pad = jnp.zeros((SUBLANES, d), F32).at[:bsz].set(c)
    ada = _ada_call(c_pad, w_ada, b_ada)[:, :bsz].reshape(depth, bsz, 6, d)
    lb_all = _bounds_call(lower_bounds)

    ci = jnp.arange(conv_ch)
    gmean = jnp.where((ci[:, None] // CONV_GROUP) == (ci[None, :] // CONV_GROUP), 1.0 / CONV_GROUP, 0.0).astype(BF16)
    ti = jnp.arange(MIX_ROWS)
    upper = (ti[:, None] < ti[None, :]).astype(BF16)

    def mixer(l, b, xin, row, after, moe_input=None):
        win, wout = w_in[l].astype(BF16), w_out[l].astype(BF16)
        wr_hi = w_router[l].astype(BF16)
        wr_lo = (w_router[l] - wr_hi.astype(F32)).astype(BF16)
        wrt = jnp.zeros((d, 256), BF16).at[:, :N_EXPERTS].set(wr_hi).at[:, 128:128 + N_EXPERTS].set(wr_lo)
        br = jnp.broadcast_to(b_router[l][:, None], (N_EXPERTS, 128))
        return _mixer_call(xin, row, ada[l, b:b + 1], norm1_g[l][None], win, conv_w[l], conv_norm_g[l][None],
                           lb_all[l][None], hgrn_norm_g[l][None], wout, gmean, norm2_g[l][None], wrt, br, upper,
                           after, moe_input)

    def moe_input_of(l, b, mixed, yg):
        return (mixed[3].T, ada[l, b:b + 1, 5:6, :], yg)

    assert bsz == 2
    m0 = mixer(0, 0, x, 0, c_pad)
    m1 = mixer(0, 1, x, 1, m0[5])
    for l in range(depth):
        ys0, yg0 = _moe_rows(l, m0[1], m0[2], m0[4], m0[5], w_gu, b_gu, w_down, b_down, m1[5])
        ys1, yg1 = _moe_rows(l, m1[1], m1[2], m1[4], m1[5], w_gu, b_gu, w_down, b_down, ys0)
        if l + 1 < depth:
            n0 = mixer(l + 1, 0, m0[0], 0, ys1, moe_input_of(l, 0, m0, yg0))
            n1 = mixer(l + 1, 1, m1[0], 0, n0[5], moe_input_of(l, 1, m1, yg1))
            m0, m1 = n0, n1
    out = ys1
    for b, (mixed, yg) in enumerate(((m0, yg0), (m1, yg1))):
        gates_t, g2, _ = moe_input_of(depth - 1, b, mixed, yg)
        out = _combine_call(mixed[0].reshape(seq, d), gates_t, g2.reshape(1, d), final_g[None], yg, out,
                            row=b, n_rows=bsz, final_norm=True)
    return out
```

```python
import functools

import jax
import jax.numpy as jnp
from jax import lax
from jax.experimental import pallas as pl
from jax.experimental.pallas import tpu as pltpu
from jax.experimental.pallas import tpu_sc as plsc

F32 = jnp.float32
BF16 = jnp.bfloat16
I32 = jnp.int32
HIGHEST = lax.Precision.HIGHEST

RMS_EPS = 1e-6
N_EXPERTS = 32
TOP_K = 4
CONV_GROUP = 64
HEAD_DIM = 128
SWIGLU_LIMIT = 7.0
SWIGLU_ALPHA = 1.702
SUBLANES = 8
VMEM_LIMIT = 56 * 1024 * 1024

MIX_ROWS = 256
EXPERT_ROWS = 512
COMBINE_ROWS = 512
SC_CORES = 2
SC_SUBCORES = 16
SC_WORKERS = SC_CORES * SC_SUBCORES
SC_CHUNK = 64
ADA_COLS = 1536
WEIGHT_CAST_ROWS = 128


def _dot(a, b):
    return jnp.dot(a, b, preferred_element_type=F32)


def _dot_nt(a, b, precision=None):
    return lax.dot_general(a, b, (((1,), (1,)), ((), ())), precision=precision,
                           preferred_element_type=F32)


def _dot_tn(a, b):
    return lax.dot_general(a, b, (((0,), (0,)), ((), ())), preferred_element_type=F32)


def _sigmoid(x):
    return 1.0 / (1.0 + jnp.exp(-x))


def _rms(x):
    return x * lax.rsqrt(jnp.mean(x * x, axis=-1, keepdims=True) + RMS_EPS)


def _pack_bf16_pair(lo, hi):
    lo_bits = lax.bitcast_convert_type(lo.astype(BF16).astype(F32), I32)
    hi_bits = lax.bitcast_convert_type(hi.astype(BF16).astype(F32), I32)
    return lax.shift_right_logical(lo_bits, 16) | (hi_bits & jnp.int32(-65536))


def _unpack_bf16_pair(p):
    lo = lax.bitcast_convert_type(lax.shift_left(p, 16), F32)
    hi = lax.bitcast_convert_type(p & jnp.int32(-65536), F32)
    return lo, hi


def _ada_kernel(c_ref, w_ref, b_ref, o_ref):
    cv = c_ref[...]
    ca = cv * _sigmoid(cv)
    o_ref[0] = jnp.dot(ca, w_ref[0], precision=HIGHEST, preferred_element_type=F32) + b_ref[0]


def _ada_call(c_pad, w_ada, b_ada):
    depth, d, n6 = w_ada.shape
    rows = c_pad.shape[0]
    return pl.pallas_call(
        _ada_kernel,
        grid=(depth, n6 // ADA_COLS),
        in_specs=[
            pl.BlockSpec((rows, d), lambda l, j: (0, 0)),
            pl.BlockSpec((1, d, ADA_COLS), lambda l, j: (l, 0, j)),
            pl.BlockSpec((1, 1, ADA_COLS), lambda l, j: (l, 0, j)),
        ],
        out_specs=pl.BlockSpec((1, rows, ADA_COLS), lambda l, j: (l, 0, j)),
        out_shape=jax.ShapeDtypeStruct((depth, rows, n6), F32),
        compiler_params=pltpu.CompilerParams(
            dimension_semantics=("arbitrary", "arbitrary"), vmem_limit_bytes=VMEM_LIMIT),
        name="ada_proj",
    )(c_pad, w_ada, b_ada.reshape(depth, 1, n6))


def _bounds_kernel(lb_ref, o_ref):
    rows = [lb_ref[l:l + 1, :] for l in range(lb_ref.shape[0])]
    m = functools.reduce(jnp.maximum, rows)
    es = [jnp.exp(r - m) for r in rows]
    tot = functools.reduce(lambda a, b: a + b, es)
    cum = None
    first = None
    for l, e in enumerate(es):
        p = e / tot
        cum = p if cum is None else cum + p
        if first is None:
            first = cum
        o_ref[l:l + 1, :] = cum - first


def _bounds_call(lower_bounds):
    return pl.pallas_call(
        _bounds_kernel,
        out_shape=jax.ShapeDtypeStruct(lower_bounds.shape, F32),
        name="hgrn_bounds",
    )(lower_bounds.astype(F32))


def _small_level_exponents(g, row):
    t_rows, width = g.shape
    blocks = t_rows // SUBLANES

    def blockroll(x, k):
        return pltpu.roll(x.reshape(blocks, SUBLANES, width), k, 1).reshape(t_rows, width)

    a1 = g + blockroll(g, 1)
    a2 = a1 + blockroll(g, 2)
    a3 = a2 + blockroll(g, 3)
    b1 = blockroll(g, SUBLANES - 1)
    b2 = b1 + blockroll(g, SUBLANES - 2)
    b3 = b2 + blockroll(g, SUBLANES - 3)
    o = row & 7
    o2 = row & 3
    arg1 = jnp.where((row & 1) != 0, g, 0.0)
    arg2 = jnp.where(o2 == 3, a1, jnp.where(o2 == 2, g, jnp.where(o2 == 0, b1, 0.0)))
    up4 = jnp.where(o == 7, a3, jnp.where(o == 6, a2, jnp.where(o == 5, a1, g)))
    lo4 = jnp.where(o == 0, b3, jnp.where(o == 1, b2, jnp.where(o == 2, b1, 0.0)))
    arg4 = jnp.where(o >= 4, up4, lo4)
    return {1: arg1, 2: arg2, 4: arg4}


def _hgrn_head(q, k, v, g_cum, small_args, group_masks, st_ref, h, row):
    t_rows = q.shape[0]
    scores = jnp.zeros((t_rows, t_rows), F32)
    half = t_rows // 2
    while half >= 1:
        grp = 2 * half
        upper = (row & half) != 0
        if half >= SUBLANES:
            g_mid = g_cum.reshape(t_rows // grp, grp, HEAD_DIM)[:, half - 1:half, :]
            g_mid = jnp.broadcast_to(g_mid, (t_rows // grp, grp, HEAD_DIM)).reshape(t_rows, HEAD_DIM)
            d = g_cum - g_mid
            arg = jnp.where(upper, d, -d)
        else:
            arg = small_args[half]
        e = jnp.exp(arg)
        qh = jnp.where(upper, q * e, 0.0).astype(BF16)
        kh = jnp.where(upper, 0.0, k * e).astype(BF16)
        sc = _dot_nt(qh, kh)
        if grp < t_rows:
            sc = jnp.where(group_masks[grp], sc, 0.0)
        scores = scores + sc
        half //= 2

    out = _dot(scores.astype(BF16), v.astype(BF16)) + jnp.sum(q * k, axis=-1, keepdims=True) * v

    st = st_ref[h]
    out = out + _dot_nt((q * jnp.exp(g_cum)).astype(BF16), st.astype(BF16))
    g_last = g_cum[t_rows - 1:t_rows, :]
    kd = (k * jnp.exp(g_last - g_cum)).astype(BF16)
    st_ref[h] = st * jnp.exp(g_last) + _dot_tn(v.astype(BF16), kd)
    return out


def _moe_residual(x, gate_cols, g2, yg_ref):
    t_rows, d = x.shape
    acc_lo = jnp.zeros((t_rows, d // 2), F32)
    acc_hi = jnp.zeros((t_rows, d // 2), F32)
    for kk in range(TOP_K):
        lo, hi = _unpack_bf16_pair(yg_ref[kk])
        gk = gate_cols[:, kk:kk + 1]
        acc_lo = acc_lo + gk * lo
        acc_hi = acc_hi + gk * hi
    return x + g2 * jnp.concatenate([acc_lo, acc_hi], axis=1)


def _mixer_kernel(*refs, conv_ch, hgrn_w, with_moe_input):
    if with_moe_input:
        pgate_ref, pg2_ref, yg_ref = refs[0:3]
        refs = refs[3:]
    (x_ref, ada_ref, n1g_ref, win_ref, convw_ref, cng_ref, lb_ref, hng_ref, wout_ref,
     gmean_ref, n2g_ref, wrt_ref, br_ref, upper_ref, after_ref,
     xo_ref, hn2_ref, eidx_ref, gate_ref, rank_ref, cnt_ref,
     st_ref, carry_ref, base_ref) = refs
    del after_ref
    b = pl.program_id(0)
    i = pl.program_id(1)
    t_rows = x_ref.shape[1]
    heads = hgrn_w // HEAD_DIM

    @pl.when(i == 0)
    def _():
        st_ref[...] = jnp.zeros_like(st_ref)
        carry_ref[...] = jnp.zeros_like(carry_ref)

    @pl.when((b == 0) & (i == 0))
    def _():
        base_ref[...] = jnp.zeros_like(base_ref)

    sh1, sc1, g1 = ada_ref[0, 0:1, :], ada_ref[0, 1:2, :], ada_ref[0, 2:3, :]
    sh2, sc2, g2 = ada_ref[0, 3:4, :], ada_ref[0, 4:5, :], ada_ref[0, 5:6, :]
    del g2

    x = x_ref[0]
    if with_moe_input:
        x = _moe_residual(x, pgate_ref[...], pg2_ref[0], yg_ref)
    hn = (_rms(x) * n1g_ref[...]) * (1.0 + sc1) + sh1
    proj = _dot(hn.astype(BF16), win_ref[...])
    c0 = conv_ch
    cb, cc, ch = proj[:, 0:c0], proj[:, c0:2 * c0], proj[:, 2 * c0:3 * c0]
    o0 = 3 * c0
    q_raw = proj[:, o0:o0 + hgrn_w]
    f_raw = proj[:, o0 + hgrn_w:o0 + 2 * hgrn_w]
    v_all = proj[:, o0 + 2 * hgrn_w:o0 + 3 * hgrn_w]
    og = proj[:, o0 + 3 * hgrn_w:o0 + 4 * hgrn_w]

    u = cc * ch
    carry = carry_ref[...]
    row8 = lax.broadcasted_iota(I32, (SUBLANES, c0), 0)
    u1 = pltpu.roll(u, 1, 0)
    u2 = pltpu.roll(u, 2, 0)
    u1 = jnp.concatenate([jnp.where(row8 < 1, pltpu.roll(carry, 1, 0), u1[0:SUBLANES]), u1[SUBLANES:]], axis=0)
    u2 = jnp.concatenate([jnp.where(row8 < 2, pltpu.roll(carry, 2, 0), u2[0:SUBLANES]), u2[SUBLANES:]], axis=0)
    carry_ref[...] = u[t_rows - SUBLANES:t_rows]
    yc = cb * (convw_ref[0:1, :] * u2 + convw_ref[1:2, :] * u1 + convw_ref[2:3, :] * u)
    sq = yc * yc
    sq_hi = sq.astype(BF16)
    sq_lo = (sq - sq_hi.astype(F32)).astype(BF16)
    gms = _dot(sq_hi, gmean_ref[...]) + _dot(sq_lo, gmean_ref[...])
    ycn = yc * lax.rsqrt(gms + RMS_EPS) * cng_ref[...]

    lb = lb_ref[...]
    log_lb = jnp.log(lb)
    log_1mlb = jnp.log1p(-lb)
    log_sig = jnp.minimum(f_raw, 0.0) - jnp.log(1.0 + jnp.exp(-jnp.abs(f_raw)))
    bb = log_1mlb + log_sig
    log_f = jnp.maximum(log_lb, bb) + jnp.log(1.0 + jnp.exp(-jnp.abs(log_lb - bb)))
    k_all = 1.0 - jnp.exp(log_f)
    q_all = q_raw * _sigmoid(q_raw)
    roww = lax.broadcasted_iota(I32, (t_rows, hgrn_w), 0)
    g_cum = log_f
    s = 1
    while s < t_rows:
        g_cum = g_cum + jnp.where(roww >= s, pltpu.roll(g_cum, s, 0), 0.0)
        s *= 2
    small = _small_level_exponents(log_f, roww)
    row = lax.broadcasted_iota(I32, (t_rows, HEAD_DIM), 0)
    rowsq = lax.broadcasted_iota(I32, (t_rows, t_rows), 0)
    col = lax.broadcasted_iota(I32, (t_rows, t_rows), 1)
    group_masks = {}
    grp = 2
    while grp < t_rows:
        shift = grp.bit_length() - 1
        group_masks[grp] = lax.shift_right_logical(rowsq, shift) == lax.shift_right_logical(col, shift)
        grp *= 2
    outs = []
    for h in range(heads):
        sl = slice(h * HEAD_DIM, (h + 1) * HEAD_DIM)
        small_h = {lvl: a[:, sl] for lvl, a in small.items()}
        o = _hgrn_head(q_all[:, sl], k_all[:, sl], v_all[:, sl], g_cum[:, sl], small_h, group_masks, st_ref, h,
                       row)
        outs.append(_rms(o))
    oh = jnp.concatenate(outs, axis=1) * hng_ref[...] * (og * _sigmoid(og))

    mix = _dot(ycn.astype(BF16), wout_ref[0:c0, :]) + _dot(oh.astype(BF16), wout_ref[c0:c0 + hgrn_w, :])
    xn = x + g1 * mix
    xo_ref[0] = xn

    hn2 = (_rms(xn) * n2g_ref[...]) * (1.0 + sc2) + sh2
    dh = hn2.shape[1] // 2
    hn2_ref[...] = _pack_bf16_pair(hn2[:, 0:dh], hn2[:, dh:2 * dh])
    h_hi = hn2.astype(BF16)
    h_lo = (hn2 - h_hi.astype(F32)).astype(BF16)
    prod = _dot(h_hi, wrt_ref[...])
    logits_tok = prod[:, 0:128] + prod[:, 128:256] + _dot(h_lo, wrt_ref[:, 0:128])
    logits = logits_tok.T[0:N_EXPERTS, :] + br_ref[:, 0:1]
    eio = lax.broadcasted_iota(I32, logits.shape, 0)
    order = jnp.zeros(logits.shape, F32)
    for e2 in range(N_EXPERTS):
        other = logits[e2:e2 + 1, :]
        beats = (other > logits) | ((other == logits) & (eio > e2))
        order = order + beats.astype(F32)
    eio_f = eio.astype(F32)
    vals, sels = [], []
    for kk in range(TOP_K):
        sel = order == float(kk)
        vals.append(jnp.sum(jnp.where(sel, logits, 0.0), axis=0, keepdims=True))
        sels.append(sel)
        eidx_ref[kk:kk + 1, :] = jnp.sum(jnp.where(sel, eio_f, 0.0), axis=0, keepdims=True).astype(I32)
    exps = [jnp.exp(vv - vals[0]) for vv in vals]
    tot = functools.reduce(lambda a, c: a + c, exps)
    member = functools.reduce(lambda a, c: a + c, [s_.astype(F32) for s_ in sels])
    before = _dot(member.astype(BF16), upper_ref[...])
    base = base_ref[:, 0:1]
    slot = before + base
    for kk in range(TOP_K):
        gate_ref[kk:kk + 1, :] = exps[kk] / tot
        rank_ref[kk:kk + 1, :] = jnp.sum(jnp.where(sels[kk], slot, 0.0), axis=0, keepdims=True).astype(I32)
    base_new = base + jnp.sum(member, axis=1, keepdims=True)
    base_ref[...] = jnp.broadcast_to(base_new, base_ref.shape)
    cnt_ref[...] = jnp.broadcast_to(base_new, cnt_ref.shape)


def _mixer_call(x, row, ada_l, n1g, win, convw, cng, lb, hng, wout, gmean, n2g, wrt, br, upper, after,
                moe_input=None):
    _, seq, d = x.shape
    bsz = 1
    t = MIX_ROWS
    n_tok = bsz * seq
    conv_ch = convw.shape[1]
    hgrn_w = lb.shape[1]
    heads = hgrn_w // HEAD_DIM
    steps = seq // t
    full = lambda a: pl.BlockSpec(a.shape, lambda b, i: (0,) * a.ndim)
    tokmap = lambda b, i: (0, b * steps + i)
    kern = functools.partial(_mixer_kernel, conv_ch=conv_ch, hgrn_w=hgrn_w, with_moe_input=moe_input is not None)
    moe_specs, moe_args = [], ()
    if moe_input is not None:
        moe_specs = [
            pl.BlockSpec((t, TOP_K), lambda b, i: (i, 0)),
            pl.BlockSpec((1, 1, d), lambda b, i: (0, 0, 0)),
            pl.BlockSpec((TOP_K, t, d // 2), lambda b, i: (0, i, 0)),
        ]
        moe_args = tuple(moe_input)
    return pl.pallas_call(
        kern,
        grid=(bsz, steps),
        in_specs=moe_specs + [
            pl.BlockSpec((1, t, d), lambda b, i: (row, i, 0)),
            pl.BlockSpec((1, 6, d), lambda b, i: (b, 0, 0)),
            full(n1g), full(win), full(convw), full(cng), full(lb), full(hng), full(wout),
            full(gmean), full(n2g), full(wrt), full(br), full(upper),
            pl.BlockSpec(memory_space=pl.ANY),
        ],
        out_specs=[
            pl.BlockSpec((1, t, d), lambda b, i: (b, i, 0)),
            pl.BlockSpec((t, d // 2), lambda b, i: (b * steps + i, 0)),
            pl.BlockSpec((TOP_K, t), tokmap),
            pl.BlockSpec((TOP_K, t), tokmap),
            pl.BlockSpec((TOP_K, t), tokmap),
            pl.BlockSpec((N_EXPERTS, 128), lambda b, i: (0, 0)),
        ],
        out_shape=[
            jax.ShapeDtypeStruct((bsz, seq, d), F32),
            jax.ShapeDtypeStruct((n_tok, d // 2), I32),
            jax.ShapeDtypeStruct((TOP_K, n_tok), I32),
            jax.ShapeDtypeStruct((TOP_K, n_tok), F32),
            jax.ShapeDtypeStruct((TOP_K, n_tok), I32),
            jax.ShapeDtypeStruct((N_EXPERTS, 128), F32),
        ],
        scratch_shapes=[
            pltpu.VMEM((heads, HEAD_DIM, HEAD_DIM), F32),
            pltpu.VMEM((SUBLANES, conv_ch), F32),
            pltpu.VMEM((N_EXPERTS, 128), F32),
        ],
        compiler_params=pltpu.CompilerParams(
            dimension_semantics=("arbitrary", "arbitrary"), vmem_limit_bytes=VMEM_LIMIT),
        name="token_mixer",
    )(*moe_args, x, ada_l, n1g, win, convw, cng, lb, hng, wout, gmean, n2g, wrt, br, upper, after)


def _sc_worker():
    return lax.axis_index("s") * SC_CORES + lax.axis_index("c")


def _sc_mesh():
    return plsc.VectorSubcoreMesh(core_axis_name="c", subcore_axis_name="s")


def _sc_scatter_rows(src, pos3d, p_rows):
    kk, n_idx_rows, ch = pos3d.shape
    d = src.shape[1]
    n_ch = n_idx_rows // SC_WORKERS
    assert n_idx_rows % SC_WORKERS == 0 and n_ch % 2 == 0

    @functools.partial(
        pl.kernel, mesh=_sc_mesh(), out_type=jax.ShapeDtypeStruct((p_rows, d), src.dtype),
        scratch_types=[pltpu.VMEM((kk, n_ch, ch), I32), pltpu.VMEM((2, ch, d), src.dtype),
                       pltpu.SemaphoreType.DMA((2,)), pltpu.SemaphoreType.DMA((2,))],
        name="sc_scatter_rows")
    def scatter_kernel(src_hbm, pos_hbm, out_hbm, idx_v, rows_v, lsem, ssem):
        wid = _sc_worker()
        for j in range(kk):
            pltpu.sync_copy(pos_hbm.at[j, pl.ds(wid * n_ch, n_ch)], idx_v.at[j])
        base = wid * (n_ch * ch)

        def load(c, slot):
            return pltpu.make_async_copy(src_hbm.at[pl.ds(base + c * ch, ch)], rows_v.at[slot], lsem.at[slot])

        def scatter(c, slot, j):
            return pltpu.make_async_copy(rows_v.at[slot], out_hbm.at[idx_v.at[j, c]], ssem.at[slot])

        load(0, 0).start()

        @pl.loop(0, n_ch, step=2)
        def _(c0):
            for slot in range(2):
                c = c0 + slot
                other = 1 - slot

                @pl.when(c >= 1)
                def _():
                    for j in range(kk):
                        scatter(c - 1, other, j).wait()

                @pl.when(c + 1 < n_ch)
                def _():
                    load(c + 1, other).start()

                load(c, slot).wait()
                for j in range(kk):
                    scatter(c, slot, j).start()

        for j in range(kk):
            scatter(n_ch - 1, (n_ch - 1) % 2, j).wait()

    return scatter_kernel(src, pos3d)


def _sc_gather_rows(table, idx2d):
    n_idx_rows, ch = idx2d.shape
    d = table.shape[1]
    n_ch = n_idx_rows // SC_WORKERS
    assert n_idx_rows % SC_WORKERS == 0 and n_ch % 2 == 0

    @functools.partial(
        pl.kernel, mesh=_sc_mesh(), out_type=jax.ShapeDtypeStruct((n_idx_rows * ch, d), table.dtype),
        scratch_types=[pltpu.VMEM((n_ch, ch), I32), pltpu.VMEM((2, ch, d), table.dtype),
                       pltpu.SemaphoreType.DMA((2,)), pltpu.SemaphoreType.DMA((2,))],
        name="sc_gather_rows")
    def gather_kernel(table_hbm, idx_hbm, out_hbm, idx_v, rows_v, gsem, wsem):
        wid = _sc_worker()
        pltpu.sync_copy(idx_hbm.at[pl.ds(wid * n_ch, n_ch)], idx_v)
        base = wid * (n_ch * ch)

        def gather(c, slot):
            return pltpu.make_async_copy(table_hbm.at[idx_v.at[c]], rows_v.at[slot], gsem.at[slot])

        def write(c, slot):
            return pltpu.make_async_copy(rows_v.at[slot], out_hbm.at[pl.ds(base + c * ch, ch)], wsem.at[slot])

        gather(0, 0).start()

        @pl.loop(0, n_ch, step=2)
        def _(c0):
            for slot in range(2):
                c = c0 + slot
                other = 1 - slot

                @pl.when(c >= 1)
                def _():
                    write(c - 1, other).wait()

                @pl.when(c + 1 < n_ch)
                def _():
                    gather(c + 1, other).start()

                gather(c, slot).wait()
                write(c, slot).start()

        write(n_ch - 1, (n_ch - 1) % 2).wait()

    return gather_kernel(table, idx2d)


def _expert_kernel(eid_ref, nvalid_ref, first_ref, next_ref, slot_ref, xs_ref, wgu_hbm, bgu_ref, wdn_hbm, bdn_ref,
                   after_ref, ys_ref, wgu_st, wdn_st, wgu_bf, wdn_bf, sem_gu, sem_dn, *, layer):
    del after_ref
    i = pl.program_id(0)
    nvalid = nvalid_ref[i]
    active = nvalid > 0

    def weight_copies(expert, slot):
        return (pltpu.make_async_copy(wgu_hbm.at[layer, expert], wgu_st.at[slot], sem_gu.at[slot]),
                pltpu.make_async_copy(wdn_hbm.at[layer, expert], wdn_st.at[slot], sem_dn.at[slot]))

    @pl.when(i == 0)
    def _():
        for cp in weight_copies(eid_ref[0], slot_ref[0]):
            cp.start()

    @pl.when(first_ref[i] == 1)
    def _():
        slot = slot_ref[i]
        for cp in weight_copies(eid_ref[i], slot):
            cp.wait()

        @pl.when(next_ref[i] >= 0)
        def _():
            for cp in weight_copies(next_ref[i], 1 - slot):
                cp.start()

        def cast_rows(r, carry):
            rows = pl.ds(pl.multiple_of(r * WEIGHT_CAST_ROWS, WEIGHT_CAST_ROWS), WEIGHT_CAST_ROWS)
            wgu_bf[rows, :] = wgu_st[slot, rows, :].astype(BF16)
            wdn_bf[rows, :] = wdn_st[slot, rows, :].astype(BF16)
            return carry
        lax.fori_loop(0, wgu_bf.shape[0] // WEIGHT_CAST_ROWS, cast_rows, 0)

    def mlp_rows(n_rows):
        packed = xs_ref[0:n_rows, :]
        keep = lax.broadcasted_iota(I32, packed.shape, 0) < nvalid
        lo, hi = _unpack_bf16_pair(jnp.where(keep, packed, 0))
        xb = jnp.concatenate([lo.astype(BF16), hi.astype(BF16)], axis=1)
        hgu = _dot(xb, wgu_bf[...]) + bgu_ref[0, 0]
        f = hgu.shape[1] // 2
        a = jnp.minimum(hgu[:, 0:f], SWIGLU_LIMIT)
        g = jnp.clip(hgu[:, f:2 * f], -SWIGLU_LIMIT, SWIGLU_LIMIT)
        act = a * _sigmoid(SWIGLU_ALPHA * a) * (g + 1.0)
        y = _dot(act.astype(BF16), wdn_bf[...]) + bdn_ref[0, 0]
        do = y.shape[1] // 2
        ys_ref[0:n_rows, :] = _pack_bf16_pair(y[:, 0:do], y[:, do:2 * do])

    full_rows = xs_ref.shape[0]
    half_rows = full_rows // 2

    @pl.when(nvalid > half_rows)
    def _():
        mlp_rows(full_rows)

    @pl.when(active & (nvalid <= half_rows))
    def _():
        mlp_rows(half_rows)
        ys_ref[half_rows:full_rows, :] = jnp.zeros((full_rows - half_rows, ys_ref.shape[1]), ys_ref.dtype)

    @pl.when(jnp.logical_not(active))
    def _():
        ys_ref[...] = jnp.zeros_like(ys_ref)


def _expert_call(layer, tables, xs, wgu, bgu, wdn, bdn, after):
    p_rows, dh = xs.shape
    bm = EXPERT_ROWS
    depth, n_e, d, f2 = wgu.shape
    dout = wdn.shape[3]
    assert f2 // 2 == d and d % WEIGHT_CAST_ROWS == 0
    bias_map = lambda i, eid, *_: (layer, eid[i], 0, 0)
    grid_spec = pltpu.PrefetchScalarGridSpec(
        num_scalar_prefetch=len(tables),
        grid=(p_rows // bm,),
        in_specs=[
            pl.BlockSpec((bm, dh), lambda i, *_: (i, 0)),
            pl.BlockSpec(memory_space=pl.ANY),
            pl.BlockSpec((1, 1, 1, f2), bias_map),
            pl.BlockSpec(memory_space=pl.ANY),
            pl.BlockSpec((1, 1, 1, dout), bias_map),
            pl.BlockSpec(memory_space=pl.ANY),
        ],
        out_specs=pl.BlockSpec((bm, dout // 2), lambda i, *_: (i, 0)),
        scratch_shapes=[
            pltpu.VMEM((2, d, f2), F32), pltpu.VMEM((2, f2 // 2, dout), F32),
            pltpu.VMEM((d, f2), BF16), pltpu.VMEM((f2 // 2, dout), BF16),
            pltpu.SemaphoreType.DMA((2,)), pltpu.SemaphoreType.DMA((2,)),
        ],
    )
    return pl.pallas_call(
        functools.partial(_expert_kernel, layer=layer),
        grid_spec=grid_spec,
        out_shape=jax.ShapeDtypeStruct((p_rows, dout // 2), I32),
        compiler_params=pltpu.CompilerParams(
            dimension_semantics=("arbitrary",), vmem_limit_bytes=VMEM_LIMIT),
        name="moe_experts",
    )(*tables, xs, wgu, bgu.reshape(depth, n_e, 1, f2), wdn, bdn.reshape(depth, n_e, 1, dout), after)


def _combine_kernel(x_ref, gate_ref, g2_ref, fg_ref, yg_ref, after_ref, o_ref, *, final_norm):
    del after_ref
    out = _moe_residual(x_ref[...], gate_ref[...], g2_ref[...], yg_ref)
    if final_norm:
        out = _rms(out) * fg_ref[...]
    o_ref[...] = out


def _combine_call(x2d, gates_t, g2, final_g, yg, after, row, n_rows, final_norm):
    n_tok, d = x2d.shape
    t = COMBINE_ROWS
    kern = functools.partial(_combine_kernel, final_norm=final_norm)
    return pl.pallas_call(
        kern,
        grid=(n_tok // t,),
        in_specs=[
            pl.BlockSpec((t, d), lambda i: (i, 0)),
            pl.BlockSpec((t, TOP_K), lambda i: (i, 0)),
            pl.BlockSpec((1, d), lambda i: (0, 0)),
            pl.BlockSpec((1, d), lambda i: (0, 0)),
            pl.BlockSpec((TOP_K, t, d // 2), lambda i: (0, i, 0)),
            pl.BlockSpec(memory_space=pl.ANY),
        ],
        out_specs=pl.BlockSpec((None, t, d), lambda i: (row, i, 0)),
        out_shape=jax.ShapeDtypeStruct((n_rows, n_tok, d), F32),
        input_output_aliases={5: 0} if row > 0 else {},
        compiler_params=pltpu.CompilerParams(
            dimension_semantics=("arbitrary",), vmem_limit_bytes=VMEM_LIMIT),
        name="moe_combine",
    )(x2d, gates_t, g2, final_g, yg, after)


def _pos_kernel(pstart_ref, eidx_ref, rank_ref, pos_ref):
    e = eidx_ref[...]
    pos = rank_ref[...]
    for j in range(N_EXPERTS):
        pos = pos + jnp.where(e == j, pstart_ref[j], 0)
    pos_ref[...] = pos


def _pos_call(pstarts, eidx, rank):
    spec = pl.BlockSpec(eidx.shape, lambda i, ps: (0, 0))
    return pl.pallas_call(
        _pos_kernel,
        grid_spec=pltpu.PrefetchScalarGridSpec(num_scalar_prefetch=1, grid=(1,), in_specs=[spec, spec],
                                               out_specs=spec),
        out_shape=jax.ShapeDtypeStruct(eidx.shape, I32),
        name="moe_positions",
    )(pstarts, eidx, rank)


def _route_tables(eidx, rank, cnt, n_blocks):
    bm = EXPERT_ROWS
    counts = cnt[:, 0].astype(I32)
    padded = ((counts + bm - 1) // bm) * bm
    pends = jnp.cumsum(padded)
    pstarts = pends - padded
    pos = _pos_call(pstarts, eidx, rank)
    block_row = jnp.arange(n_blocks, dtype=I32) * bm
    block_eid = jnp.minimum(jnp.sum((pends[None, :] <= block_row[:, None]).astype(I32), axis=1), N_EXPERTS - 1)
    experts = jnp.arange(N_EXPERTS, dtype=I32)
    onehot = block_eid[:, None] == experts[None, :]
    row_end = jnp.sum(jnp.where(onehot, (pstarts + counts)[None, :], 0), axis=1)
    block_valid = jnp.clip(row_end - block_row, 0, bm).astype(I32)
    active = block_valid > 0
    prev_eid = jnp.concatenate([jnp.full((1,), -1, I32), block_eid[:-1]])
    first = (active & (block_eid != prev_eid)).astype(I32)
    later = (experts[None, :] > experts[:, None]) & (counts[None, :] > 0)
    next_expert = jnp.min(jnp.where(later, experts[None, :], N_EXPERTS), axis=1)
    next_expert = jnp.where(next_expert == N_EXPERTS, -1, next_expert)
    block_next = jnp.sum(jnp.where(onehot, next_expert[None, :], 0), axis=1).astype(I32)
    slot = ((jnp.cumsum(first) - 1) % 2).astype(I32)
    return pos, (block_eid, block_valid, first, block_next, slot)


def _moe_rows(layer, hn2p, eidx, rank, cnt, wgu, bgu, wdn, bdn, after):
    n_tok, dh = hn2p.shape
    n_blocks = n_tok * TOP_K // EXPERT_ROWS + N_EXPERTS
    pos, tables = _route_tables(eidx, rank, cnt, n_blocks)
    xs = _sc_scatter_rows(hn2p, pos.reshape(TOP_K, n_tok // SC_CHUNK, SC_CHUNK), n_blocks * EXPERT_ROWS)
    ys = _expert_call(layer, tables, xs, wgu, bgu, wdn, bdn, after)
    yg = _sc_gather_rows(ys, pos.reshape(TOP_K * n_tok // SC_CHUNK, SC_CHUNK))
    return ys, yg.reshape(TOP_K, n_tok, dh)


def kernel(x, c, w_ada, b_ada, norm1_g, norm2_g, w_in, conv_w, conv_norm_g, lower_bounds, hgrn_norm_g, w_out,
           w_router, b_router, w_gu, b_gu, w_down, b_down, final_g):
    depth = w_ada.shape[0]
    bsz, seq, d = x.shape
    conv_ch = conv_w.shape[2]
    c_pad = jnp.zeros((SUBLANES, d), F32).at[:bsz].set(c)
    ada = _ada_call(c_pad, w_ada, b_ada)[:, :bsz].reshape(depth, bsz, 6, d)
    lb_all = _bounds_call(lower_bounds)

    ci = jnp.arange(conv_ch)
    gmean = jnp.where((ci[:, None] // CONV_GROUP) == (ci[None, :] // CONV_GROUP), 1.0 / CONV_GROUP, 0.0).astype(BF16)
    ti = jnp.arange(MIX_ROWS)
    upper = (ti[:, None] < ti[None, :]).astype(BF16)

    def mixer(l, b, xin, row, after, moe_input=None):
        win, wout = w_in[l].astype(BF16), w_out[l].astype(BF16)
        wr_hi = w_router[l].astype(BF16)
        wr_lo = (w_router[l] - wr_hi.astype(F32)).astype(BF16)
        wrt = jnp.zeros((d, 256), BF16).at[:, :N_EXPERTS].set(wr_hi).at[:, 128:128 + N_EXPERTS].set(wr_lo)
        br = jnp.broadcast_to(b_router[l][:, None], (N_EXPERTS, 128))
        return _mixer_call(xin, row, ada[l, b:b + 1], norm1_g[l][None], win, conv_w[l], conv_norm_g[l][None],
                           lb_all[l][None], hgrn_norm_g[l][None], wout, gmean, norm2_g[l][None], wrt, br, upper,
                           after, moe_input)

    def moe_input_of(l, b, mixed, yg):
        return (mixed[3].T, ada[l, b:b + 1, 5:6, :], yg)

    assert bsz == 2
    m0 = mixer(0, 0, x, 0, c_pad)
    m1 = mixer(0, 1, x, 1, m0[5])
    for l in range(depth):
        ys0, yg0 = _moe_rows(l, m0[1], m0[2], m0[4], m0[5], w_gu, b_gu, w_down, b_down, m1[5])
        ys1, yg1 = _moe_rows(l, m1[1], m1[2], m1[4], m1[5], w_gu, b_gu, w_down, b_down, ys0)
        if l + 1 < depth:
            n0 = mixer(l + 1, 0, m0[0], 0, ys1, moe_input_of(l, 0, m0, yg0))
            n1 = mixer(l + 1, 1, m1[0], 0, n0[5], moe_input_of(l, 1, m1, yg1))
            m0, m1 = n0, n1
    out = ys1
    for b, (mixed, yg) in enumerate(((m0, yg0), (m1, yg1))):
        gates_t, g2, _ = moe_input_of(depth - 1, b, mixed, yg)
        out = _combine_call(mixed[0].reshape(seq, d), gates_t, g2.reshape(1, d), final_g[None], yg, out,
                            row=b, n_rows=bsz, final_norm=True)
    return out
```

```python
import functools

import jax
import jax.numpy as jnp
from jax import lax
from jax.experimental import pallas as pl
from jax.experimental.pallas import tpu as pltpu
from jax.experimental.pallas import tpu_sc as plsc

F32 = jnp.float32
BF16 = jnp.bfloat16
I32 = jnp.int32
HIGHEST = lax.Precision.HIGHEST

RMS_EPS = 1e-6
N_EXPERTS = 32
TOP_K = 4
CONV_GROUP = 64
HEAD_DIM = 128
SWIGLU_LIMIT = 7.0
SWIGLU_ALPHA = 1.702
SUBLANES = 8
VMEM_LIMIT = 56 * 1024 * 1024

MIX_ROWS = 256
EXPERT_ROWS = 512
COMBINE_ROWS = 512
SC_CORES = 2
SC_SUBCORES = 16
SC_WORKERS = SC_CORES * SC_SUBCORES
SC_CHUNK = 64
ADA_COLS = 1536
WEIGHT_CAST_ROWS = 128


def _dot(a, b):
    return jnp.dot(a, b, preferred_element_type=F32)


def _dot_nt(a, b):
    return lax.dot_general(a, b, (((1,), (1,)), ((), ())), preferred_element_type=F32)


def _dot_tn(a, b):
    return lax.dot_general(a, b, (((0,), (0,)), ((), ())), preferred_element_type=F32)


def _sigmoid(x):
    return 1.0 / (1.0 + jnp.exp(-x))


def _rms(x):
    return x * lax.rsqrt(jnp.mean(x * x, axis=-1, keepdims=True) + RMS_EPS)


def _pack_bf16_pair(lo, hi):
    lo_bits = lax.bitcast_convert_type(lo.astype(BF16).astype(F32), I32)
    hi_bits = lax.bitcast_convert_type(hi.astype(BF16).astype(F32), I32)
    return lax.shift_right_logical(lo_bits, 16) | (hi_bits & jnp.int32(-65536))


def _unpack_bf16_pair(p):
    lo = lax.bitcast_convert_type(lax.shift_left(p, 16), F32)
    hi = lax.bitcast_convert_type(p & jnp.int32(-65536), F32)
    return lo, hi


def _ada_kernel(c_ref, w_ref, b_ref, o_ref):
    cv = c_ref[...]
    ca = cv * _sigmoid(cv)
    o_ref[0] = jnp.dot(ca, w_ref[0], precision=HIGHEST, preferred_element_type=F32) + b_ref[0]


def _ada_call(c_pad, w_ada, b_ada):
    depth, d, n6 = w_ada.shape
    rows = c_pad.shape[0]
    return pl.pallas_call(
        _ada_kernel,
        grid=(depth, n6 // ADA_COLS),
        in_specs=[
            pl.BlockSpec((rows, d), lambda l, j: (0, 0)),
            pl.BlockSpec((1, d, ADA_COLS), lambda l, j: (l, 0, j)),
            pl.BlockSpec((1, 1, ADA_COLS), lambda l, j: (l, 0, j)),
        ],
        out_specs=pl.BlockSpec((1, rows, ADA_COLS), lambda l, j: (l, 0, j)),
        out_shape=jax.ShapeDtypeStruct((depth, rows, n6), F32),
        compiler_params=pltpu.CompilerParams(
            dimension_semantics=("arbitrary", "arbitrary"), vmem_limit_bytes=VMEM_LIMIT),
        name="ada_proj",
    )(c_pad, w_ada, b_ada.reshape(depth, 1, n6))


def _bounds_kernel(lb_ref, o_ref):
    rows = [lb_ref[l:l + 1, :] for l in range(lb_ref.shape[0])]
    m = functools.reduce(jnp.maximum, rows)
    es = [jnp.exp(r - m) for r in rows]
    tot = functools.reduce(lambda a, b: a + b, es)
    cum = None
    first = None
    for l, e in enumerate(es):
        p = e / tot
        cum = p if cum is None else cum + p
        if first is None:
            first = cum
        o_ref[l:l + 1, :] = cum - first


def _bounds_call(lower_bounds):
    return pl.pallas_call(
        _bounds_kernel,
        out_shape=jax.ShapeDtypeStruct(lower_bounds.shape, F32),
        name="hgrn_bounds",
    )(lower_bounds.astype(F32))


def _small_level_exponents(g, row):
    t_rows, width = g.shape
    blocks = t_rows // SUBLANES

    def blockroll(x, k):
        return pltpu.roll(x.reshape(blocks, SUBLANES, width), k, 1).reshape(t_rows, width)

    a1 = g + blockroll(g, 1)
    a2 = a1 + blockroll(g, 2)
    a3 = a2 + blockroll(g, 3)
    b1 = blockroll(g, SUBLANES - 1)
    b2 = b1 + blockroll(g, SUBLANES - 2)
    b3 = b2 + blockroll(g, SUBLANES - 3)
    o = row & 7
    o2 = row & 3
    arg1 = jnp.where((row & 1) != 0, g, 0.0)
    arg2 = jnp.where(o2 == 3, a1, jnp.where(o2 == 2, g, jnp.where(o2 == 0, b1, 0.0)))
    up4 = jnp.where(o == 7, a3, jnp.where(o == 6, a2, jnp.where(o == 5, a1, g)))
    lo4 = jnp.where(o == 0, b3, jnp.where(o == 1, b2, jnp.where(o == 2, b1, 0.0)))
    arg4 = jnp.where(o >= 4, up4, lo4)
    return {1: arg1, 2: arg2, 4: arg4}


def _hgrn_head(q, k, v, g_cum, small_args, group_masks, st_ref, h, row):
    t_rows = q.shape[0]
    scores = jnp.zeros((t_rows, t_rows), F32)
    half = t_rows // 2
    while half >= 1:
        grp = 2 * half
        upper = (row & half) != 0
        if half >= SUBLANES:
            g_mid = g_cum.reshape(t_rows // grp, grp, HEAD_DIM)[:, half - 1:half, :]
            g_mid = jnp.broadcast_to(g_mid, (t_rows // grp, grp, HEAD_DIM)).reshape(t_rows, HEAD_DIM)
            d = g_cum - g_mid
            arg = jnp.where(upper, d, -d)
        else:
            arg = small_args[half]
        e = jnp.exp(arg)
        qh = jnp.where(upper, q * e, 0.0).astype(BF16)
        kh = jnp.where(upper, 0.0, k * e).astype(BF16)
        sc = _dot_nt(qh, kh)
        if grp < t_rows:
            sc = jnp.where(group_masks[grp], sc, 0.0)
        scores = scores + sc
        half //= 2

    out = _dot(scores.astype(BF16), v.astype(BF16)) + jnp.sum(q * k, axis=-1, keepdims=True) * v

    st = st_ref[h]
    out = out + _dot_nt((q * jnp.exp(g_cum)).astype(BF16), st.astype(BF16))
    g_last = g_cum[t_rows - 1:t_rows, :]
    kd = (k * jnp.exp(g_last - g_cum)).astype(BF16)
    st_ref[h] = st * jnp.exp(g_last) + _dot_tn(v.astype(BF16), kd)
    return out


def _moe_residual(x, gate_cols, g2, yg_ref):
    t_rows, d = x.shape
    acc_lo = jnp.zeros((t_rows, d // 2), F32)
    acc_hi = jnp.zeros((t_rows, d // 2), F32)
    for kk in range(TOP_K):
        lo, hi = _unpack_bf16_pair(yg_ref[kk])
        gk = gate_cols[:, kk:kk + 1]
        acc_lo = acc_lo + gk * lo
        acc_hi = acc_hi + gk * hi
    return x + g2 * jnp.concatenate([acc_lo, acc_hi], axis=1)


def _mixer_kernel(*refs, conv_ch, hgrn_w, with_moe_input):
    if with_moe_input:
        pgate_ref, pg2_ref, yg_ref = refs[0:3]
        refs = refs[3:]
    (x_ref, ada_ref, n1g_ref, win_ref, convw_ref, cng_ref, lb_ref, hng_ref, wout_ref,
     gmean_ref, n2g_ref, wrt_ref, br_ref, upper_ref, after_ref,
     xo_ref, hn2_ref, eidx_ref, gate_ref, rank_ref, cnt_ref,
     st_ref, carry_ref, base_ref) = refs
    del after_ref
    i = pl.program_id(1)
    t_rows = x_ref.shape[1]
    heads = hgrn_w // HEAD_DIM

    @pl.when(i == 0)
    def _():
        st_ref[...] = jnp.zeros_like(st_ref)
        carry_ref[...] = jnp.zeros_like(carry_ref)
        base_ref[...] = jnp.zeros_like(base_ref)

    sh1, sc1, g1 = ada_ref[0, 0:1, :], ada_ref[0, 1:2, :], ada_ref[0, 2:3, :]
    sh2, sc2, g2 = ada_ref[0, 3:4, :], ada_ref[0, 4:5, :], ada_ref[0, 5:6, :]
    del g2

    x = x_ref[0]
    if with_moe_input:
        x = _moe_residual(x, pgate_ref[...], pg2_ref[0], yg_ref)
    hn = (_rms(x) * n1g_ref[...]) * (1.0 + sc1) + sh1
    proj = _dot(hn.astype(BF16), win_ref[...])
    c0 = conv_ch
    cb, cc, ch = proj[:, 0:c0], proj[:, c0:2 * c0], proj[:, 2 * c0:3 * c0]
    o0 = 3 * c0
    q_raw = proj[:, o0:o0 + hgrn_w]
    f_raw = proj[:, o0 + hgrn_w:o0 + 2 * hgrn_w]
    v_all = proj[:, o0 + 2 * hgrn_w:o0 + 3 * hgrn_w]
    og = proj[:, o0 + 3 * hgrn_w:o0 + 4 * hgrn_w]

    u = cc * ch
    carry = carry_ref[...]
    row8 = lax.broadcasted_iota(I32, (SUBLANES, c0), 0)
    u1 = pltpu.roll(u, 1, 0)
    u2 = pltpu.roll(u, 2, 0)
    u1 = jnp.concatenate([jnp.where(row8 < 1, pltpu.roll(carry, 1, 0), u1[0:SUBLANES]), u1[SUBLANES:]], axis=0)
    u2 = jnp.concatenate([jnp.where(row8 < 2, pltpu.roll(carry, 2, 0), u2[0:SUBLANES]), u2[SUBLANES:]], axis=0)
    carry_ref[...] = u[t_rows - SUBLANES:t_rows]
    yc = cb * (convw_ref[0:1, :] * u2 + convw_ref[1:2, :] * u1 + convw_ref[2:3, :] * u)
    sq = yc * yc
    sq_hi = sq.astype(BF16)
    sq_lo = (sq - sq_hi.astype(F32)).astype(BF16)
    gms = _dot(sq_hi, gmean_ref[...]) + _dot(sq_lo, gmean_ref[...])
    ycn = yc * lax.rsqrt(gms + RMS_EPS) * cng_ref[...]

    lb = lb_ref[...]
    log_lb = jnp.log(lb)
    log_1mlb = jnp.log1p(-lb)
    log_sig = jnp.minimum(f_raw, 0.0) - jnp.log(1.0 + jnp.exp(-jnp.abs(f_raw)))
    bb = log_1mlb + log_sig
    log_f = jnp.maximum(log_lb, bb) + jnp.log(1.0 + jnp.exp(-jnp.abs(log_lb - bb)))
    k_all = 1.0 - jnp.exp(log_f)
    q_all = q_raw * _sigmoid(q_raw)
    roww = lax.broadcasted_iota(I32, (t_rows, hgrn_w), 0)
    g_cum = log_f
    s = 1
    while s < t_rows:
        g_cum = g_cum + jnp.where(roww >= s, pltpu.roll(g_cum, s, 0), 0.0)
        s *= 2
    small = _small_level_exponents(log_f, roww)
    row = lax.broadcasted_iota(I32, (t_rows, HEAD_DIM), 0)
    rowsq = lax.broadcasted_iota(I32, (t_rows, t_rows), 0)
    col = lax.broadcasted_iota(I32, (t_rows, t_rows), 1)
    group_masks = {}
    grp = 2
    while grp < t_rows:
        shift = grp.bit_length() - 1
        group_masks[grp] = lax.shift_right_logical(rowsq, shift) == lax.shift_right_logical(col, shift)
        grp *= 2
    outs = []
    for h in range(heads):
        sl = slice(h * HEAD_DIM, (h + 1) * HEAD_DIM)
        small_h = {lvl: a[:, sl] for lvl, a in small.items()}
        o = _hgrn_head(q_all[:, sl], k_all[:, sl], v_all[:, sl], g_cum[:, sl], small_h, group_masks, st_ref, h,
                       row)
        outs.append(_rms(o))
    oh = jnp.concatenate(outs, axis=1) * hng_ref[...] * (og * _sigmoid(og))

    mix = _dot(ycn.astype(BF16), wout_ref[0:c0, :]) + _dot(oh.astype(BF16), wout_ref[c0:c0 + hgrn_w, :])
    xn = x + g1 * mix
    xo_ref[0] = xn

    hn2 = (_rms(xn) * n2g_ref[...]) * (1.0 + sc2) + sh2
    dh = hn2.shape[1] // 2
    hn2_ref[...] = _pack_bf16_pair(hn2[:, 0:dh], hn2[:, dh:2 * dh])
    h_hi = hn2.astype(BF16)
    h_lo = (hn2 - h_hi.astype(F32)).astype(BF16)
    prod = _dot(h_hi, wrt_ref[...])
    logits_tok = prod[:, 0:128] + prod[:, 128:256] + _dot(h_lo, wrt_ref[:, 0:128])
    logits = logits_tok.T[0:N_EXPERTS, :] + br_ref[:, 0:1]
    eio = lax.broadcasted_iota(I32, logits.shape, 0)
    order = jnp.zeros(logits.shape, F32)
    for e2 in range(N_EXPERTS):
        other = logits[e2:e2 + 1, :]
        beats = (other > logits) | ((other == logits) & (eio > e2))
        order = order + beats.astype(F32)
    eio_f = eio.astype(F32)
    vals, sels = [], []
    for kk in range(TOP_K):
        sel = order == float(kk)
        vals.append(jnp.sum(jnp.where(sel, logits, 0.0), axis=0, keepdims=True))
        sels.append(sel)
        eidx_ref[kk:kk + 1, :] = jnp.sum(jnp.where(sel, eio_f, 0.0), axis=0, keepdims=True).astype(I32)
    exps = [jnp.exp(vv - vals[0]) for vv in vals]
    tot = functools.reduce(lambda a, c: a + c, exps)
    member = functools.reduce(lambda a, c: a + c, [s_.astype(F32) for s_ in sels])
    before = _dot(member.astype(BF16), upper_ref[...])
    base = base_ref[:, 0:1]
    slot = before + base
    for kk in range(TOP_K):
        gate_ref[kk:kk + 1, :] = exps[kk] / tot
        rank_ref[kk:kk + 1, :] = jnp.sum(jnp.where(sels[kk], slot, 0.0), axis=0, keepdims=True).astype(I32)
    base_new = base + jnp.sum(member, axis=1, keepdims=True)
    base_ref[...] = jnp.broadcast_to(base_new, base_ref.shape)
    cnt_ref[...] = jnp.broadcast_to(base_new, cnt_ref.shape)


def _mixer_call(x, row, ada_l, n1g, win, convw, cng, lb, hng, wout, gmean, n2g, wrt, br, upper, after,
                moe_input=None):
    _, seq, d = x.shape
    bsz = 1
    t = MIX_ROWS
    n_tok = seq
    conv_ch = convw.shape[1]
    hgrn_w = lb.shape[1]
    heads = hgrn_w // HEAD_DIM
    steps = seq // t
    full = lambda a: pl.BlockSpec(a.shape, lambda b, i: (0,) * a.ndim)
    tokmap = lambda b, i: (0, b * steps + i)
    kern = functools.partial(_mixer_kernel, conv_ch=conv_ch, hgrn_w=hgrn_w, with_moe_input=moe_input is not None)
    moe_specs, moe_args = [], ()
    if moe_input is not None:
        moe_specs = [
            pl.BlockSpec((t, TOP_K), lambda b, i: (i, 0)),
            pl.BlockSpec((1, 1, d), lambda b, i: (0, 0, 0)),
            pl.BlockSpec((TOP_K, t, d // 2), lambda b, i: (0, i, 0)),
        ]
        moe_args = tuple(moe_input)
    return pl.pallas_call(
        kern,
        grid=(bsz, steps),
        in_specs=moe_specs + [
            pl.BlockSpec((1, t, d), lambda b, i: (row, i, 0)),
            pl.BlockSpec((1, 6, d), lambda b, i: (b, 0, 0)),
            full(n1g), full(win), full(convw), full(cng), full(lb), full(hng), full(wout),
            full(gmean), full(n2g), full(wrt), full(br), full(upper),
            pl.BlockSpec(memory_space=pl.ANY),
        ],
        out_specs=[
            pl.BlockSpec((1, t, d), lambda b, i: (b, i, 0)),
            pl.BlockSpec((t, d // 2), lambda b, i: (b * steps + i, 0)),
            pl.BlockSpec((TOP_K, t), tokmap),
            pl.BlockSpec((TOP_K, t), tokmap),
            pl.BlockSpec((TOP_K, t), tokmap),
            pl.BlockSpec((N_EXPERTS, 128), lambda b, i: (0, 0)),
        ],
        out_shape=[
            jax.ShapeDtypeStruct((bsz, seq, d), F32),
            jax.ShapeDtypeStruct((n_tok, d // 2), I32),
            jax.ShapeDtypeStruct((TOP_K, n_tok), I32),
            jax.ShapeDtypeStruct((TOP_K, n_tok), F32),
            jax.ShapeDtypeStruct((TOP_K, n_tok), I32),
            jax.ShapeDtypeStruct((N_EXPERTS, 128), F32),
        ],
        scratch_shapes=[
            pltpu.VMEM((heads, HEAD_DIM, HEAD_DIM), F32),
            pltpu.VMEM((SUBLANES, conv_ch), F32),
            pltpu.VMEM((N_EXPERTS, 128), F32),
        ],
        compiler_params=pltpu.CompilerParams(
            dimension_semantics=("arbitrary", "arbitrary"), vmem_limit_bytes=VMEM_LIMIT),
        name="token_mixer",
    )(*moe_args, x, ada_l, n1g, win, convw, cng, lb, hng, wout, gmean, n2g, wrt, br, upper, after)


def _sc_worker():
    return lax.axis_index("s") * SC_CORES + lax.axis_index("c")


def _sc_mesh():
    return plsc.VectorSubcoreMesh(core_axis_name="c", subcore_axis_name="s")


def _sc_scatter_rows(src, pos3d, p_rows):
    kk, n_idx_rows, ch = pos3d.shape
    d = src.shape[1]
    n_ch = n_idx_rows // SC_WORKERS
    assert n_idx_rows % SC_WORKERS == 0 and n_ch % 2 == 0

    @functools.partial(
        pl.kernel, mesh=_sc_mesh(), out_type=jax.ShapeDtypeStruct((p_rows, d), src.dtype),
        scratch_types=[pltpu.VMEM((kk, n_ch, ch), I32), pltpu.VMEM((2, ch, d), src.dtype),
                       pltpu.SemaphoreType.DMA((2,)), pltpu.SemaphoreType.DMA((2,))],
        name="sc_scatter_rows")
    def scatter_kernel(src_hbm, pos_hbm, out_hbm, idx_v, rows_v, lsem, ssem):
        wid = _sc_worker()
        for j in range(kk):
            pltpu.sync_copy(pos_hbm.at[j, pl.ds(wid * n_ch, n_ch)], idx_v.at[j])
        base = wid * (n_ch * ch)

        def load(c, slot):
            return pltpu.make_async_copy(src_hbm.at[pl.ds(base + c * ch, ch)], rows_v.at[slot], lsem.at[slot])

        def scatter(c, slot, j):
            return pltpu.make_async_copy(rows_v.at[slot], out_hbm.at[idx_v.at[j, c]], ssem.at[slot])

        load(0, 0).start()

        @pl.loop(0, n_ch, step=2)
        def _(c0):
            for slot in range(2):
                c = c0 + slot
                other = 1 - slot

                @pl.when(c >= 1)
                def _():
                    for j in range(kk):
                        scatter(c - 1, other, j).wait()

                @pl.when(c + 1 < n_ch)
                def _():
                    load(c + 1, other).start()

                load(c, slot).wait()
                for j in range(kk):
                    scatter(c, slot, j).start()

        for j in range(kk):
            scatter(n_ch - 1, (n_ch - 1) % 2, j).wait()

    return scatter_kernel(src, pos3d)


def _sc_gather_rows(table, idx2d):
    n_idx_rows, ch = idx2d.shape
    d = table.shape[1]
    n_ch = n_idx_rows // SC_WORKERS
    assert n_idx_rows % SC_WORKERS == 0 and n_ch % 2 == 0

    @functools.partial(
        pl.kernel, mesh=_sc_mesh(), out_type=jax.ShapeDtypeStruct((n_idx_rows * ch, d), table.dtype),
        scratch_types=[pltpu.VMEM((n_ch, ch), I32), pltpu.VMEM((2, ch, d), table.dtype),
                       pltpu.SemaphoreType.DMA((2,)), pltpu.SemaphoreType.DMA((2,))],
        name="sc_gather_rows")
    def gather_kernel(table_hbm, idx_hbm, out_hbm, idx_v, rows_v, gsem, wsem):
        wid = _sc_worker()
        pltpu.sync_copy(idx_hbm.at[pl.ds(wid * n_ch, n_ch)], idx_v)
        base = wid * (n_ch * ch)

        def gather(c, slot):
            return pltpu.make_async_copy(table_hbm.at[idx_v.at[c]], rows_v.at[slot], gsem.at[slot])

        def write(c, slot):
            return pltpu.make_async_copy(rows_v.at[slot], out_hbm.at[pl.ds(base + c * ch, ch)], wsem.at[slot])

        gather(0, 0).start()

        @pl.loop(0, n_ch, step=2)
        def _(c0):
            for slot in range(2):
                c = c0 + slot
                other = 1 - slot

                @pl.when(c >= 1)
                def _():
                    write(c - 1, other).wait()

                @pl.when(c + 1 < n_ch)
                def _():
                    gather(c + 1, other).start()

                gather(c, slot).wait()
                write(c, slot).start()

        write(n_ch - 1, (n_ch - 1) % 2).wait()

    return gather_kernel(table, idx2d)


def _expert_kernel(eid_ref, nvalid_ref, first_ref, next_ref, slot_ref, xs_ref, wgu_hbm, bgu_ref, wdn_hbm, bdn_ref,
                   after_ref, ys_ref, wgu_st, wdn_st, wgu_bf, wdn_bf, sem_gu, sem_dn, *, layer):
    del after_ref
    i = pl.program_id(0)
    nvalid = nvalid_ref[i]
    active = nvalid > 0

    def weight_copies(expert, slot):
        return (pltpu.make_async_copy(wgu_hbm.at[layer, expert], wgu_st.at[slot], sem_gu.at[slot]),
                pltpu.make_async_copy(wdn_hbm.at[layer, expert], wdn_st.at[slot], sem_dn.at[slot]))

    @pl.when(i == 0)
    def _():
        for cp in weight_copies(eid_ref[0], slot_ref[0]):
            cp.start()

    @pl.when(first_ref[i] == 1)
    def _():
        slot = slot_ref[i]
        for cp in weight_copies(eid_ref[i], slot):
            cp.wait()

        @pl.when(next_ref[i] >= 0)
        def _():
            for cp in weight_copies(next_ref[i], 1 - slot):
                cp.start()

        def cast_rows(r, carry):
            rows = pl.ds(pl.multiple_of(r * WEIGHT_CAST_ROWS, WEIGHT_CAST_ROWS), WEIGHT_CAST_ROWS)
            wgu_bf[rows, :] = wgu_st[slot, rows, :].astype(BF16)
            wdn_bf[rows, :] = wdn_st[slot, rows, :].astype(BF16)
            return carry
        lax.fori_loop(0, wgu_bf.shape[0] // WEIGHT_CAST_ROWS, cast_rows, 0)

    def mlp_rows(n_rows):
        packed = xs_ref[0:n_rows, :]
        keep = lax.broadcasted_iota(I32, packed.shape, 0) < nvalid
        lo, hi = _unpack_bf16_pair(jnp.where(keep, packed, 0))
        xb = jnp.concatenate([lo.astype(BF16), hi.astype(BF16)], axis=1)
        hgu = _dot(xb, wgu_bf[...]) + bgu_ref[0, 0]
        f = hgu.shape[1] // 2
        a = jnp.minimum(hgu[:, 0:f], SWIGLU_LIMIT)
        g = jnp.clip(hgu[:, f:2 * f], -SWIGLU_LIMIT, SWIGLU_LIMIT)
        act = a * _sigmoid(SWIGLU_ALPHA * a) * (g + 1.0)
        y = _dot(act.astype(BF16), wdn_bf[...]) + bdn_ref[0, 0]
        do = y.shape[1] // 2
        ys_ref[0:n_rows, :] = _pack_bf16_pair(y[:, 0:do], y[:, do:2 * do])

    full_rows = xs_ref.shape[0]
    lower = 0
    for n_rows in (full_rows // 4, full_rows // 2, full_rows):
        @pl.when((nvalid > lower) & (nvalid <= n_rows))
        def _(n_rows=n_rows):
            mlp_rows(n_rows)
            if n_rows < full_rows:
                ys_ref[n_rows:full_rows, :] = jnp.zeros((full_rows - n_rows, ys_ref.shape[1]), ys_ref.dtype)
        lower = n_rows

    @pl.when(jnp.logical_not(active))
    def _():
        ys_ref[...] = jnp.zeros_like(ys_ref)


def _expert_call(layer, tables, xs, wgu, bgu, wdn, bdn, after):
    p_rows, dh = xs.shape
    bm = EXPERT_ROWS
    depth, n_e, d, f2 = wgu.shape
    dout = wdn.shape[3]
    assert f2 // 2 == d and d % WEIGHT_CAST_ROWS == 0
    bias_map = lambda i, eid, *_: (layer, eid[i], 0, 0)
    grid_spec = pltpu.PrefetchScalarGridSpec(
        num_scalar_prefetch=len(tables),
        grid=(p_rows // bm,),
        in_specs=[
            pl.BlockSpec((bm, dh), lambda i, *_: (i, 0)),
            pl.BlockSpec(memory_space=pl.ANY),
            pl.BlockSpec((1, 1, 1, f2), bias_map),
            pl.BlockSpec(memory_space=pl.ANY),
            pl.BlockSpec((1, 1, 1, dout), bias_map),
            pl.BlockSpec(memory_space=pl.ANY),
        ],
        out_specs=pl.BlockSpec((bm, dout // 2), lambda i, *_: (i, 0)),
        scratch_shapes=[
            pltpu.VMEM((2, d, f2), F32), pltpu.VMEM((2, f2 // 2, dout), F32),
            pltpu.VMEM((d, f2), BF16), pltpu.VMEM((f2 // 2, dout), BF16),
            pltpu.SemaphoreType.DMA((2,)), pltpu.SemaphoreType.DMA((2,)),
        ],
    )
    return pl.pallas_call(
        functools.partial(_expert_kernel, layer=layer),
        grid_spec=grid_spec,
        out_shape=jax.ShapeDtypeStruct((p_rows, dout // 2), I32),
        compiler_params=pltpu.CompilerParams(
            dimension_semantics=("arbitrary",), vmem_limit_bytes=VMEM_LIMIT),
        name="moe_experts",
    )(*tables, xs, wgu, bgu.reshape(depth, n_e, 1, f2), wdn, bdn.reshape(depth, n_e, 1, dout), after)


def _final_kernel(x_ref, gate_ref, g2_ref, fg_ref, yg_ref, after_ref, o_ref):
    del after_ref
    out = _moe_residual(x_ref[...], gate_ref[...], g2_ref[...], yg_ref)
    o_ref[...] = _rms(out) * fg_ref[...]


def _final_call(x2d, gates_t, g2, final_g, yg, after, row, n_rows):
    n_tok, d = x2d.shape
    t = COMBINE_ROWS
    return pl.pallas_call(
        _final_kernel,
        grid=(n_tok // t,),
        in_specs=[
            pl.BlockSpec((t, d), lambda i: (i, 0)),
            pl.BlockSpec((t, TOP_K), lambda i: (i, 0)),
            pl.BlockSpec((1, d), lambda i: (0, 0)),
            pl.BlockSpec((1, d), lambda i: (0, 0)),
            pl.BlockSpec((TOP_K, t, d // 2), lambda i: (0, i, 0)),
            pl.BlockSpec(memory_space=pl.ANY),
        ],
        out_specs=pl.BlockSpec((None, t, d), lambda i: (row, i, 0)),
        out_shape=jax.ShapeDtypeStruct((n_rows, n_tok, d), F32),
        input_output_aliases={5: 0} if row > 0 else {},
        compiler_params=pltpu.CompilerParams(
            dimension_semantics=("arbitrary",), vmem_limit_bytes=VMEM_LIMIT),
        name="final_residual_norm",
    )(x2d, gates_t, g2, final_g, yg, after)


def _pos_kernel(pstart_ref, eidx_ref, rank_ref, pos_ref):
    e = eidx_ref[...]
    pos = rank_ref[...]
    for j in range(N_EXPERTS):
        pos = pos + jnp.where(e == j, pstart_ref[j], 0)
    pos_ref[...] = pos


def _pos_call(pstarts, eidx, rank):
    spec = pl.BlockSpec(eidx.shape, lambda i, ps: (0, 0))
    return pl.pallas_call(
        _pos_kernel,
        grid_spec=pltpu.PrefetchScalarGridSpec(num_scalar_prefetch=1, grid=(1,), in_specs=[spec, spec],
                                               out_specs=spec),
        out_shape=jax.ShapeDtypeStruct(eidx.shape, I32),
        name="moe_positions",
    )(pstarts, eidx, rank)


def _route_tables(eidx, rank, cnt, n_blocks):
    bm = EXPERT_ROWS
    counts = cnt[:, 0].astype(I32)
    padded = ((counts + bm - 1) // bm) * bm
    pends = jnp.cumsum(padded)
    pstarts = pends - padded
    pos = _pos_call(pstarts, eidx, rank)
    block_row = jnp.arange(n_blocks, dtype=I32) * bm
    block_eid = jnp.minimum(jnp.sum((pends[None, :] <= block_row[:, None]).astype(I32), axis=1), N_EXPERTS - 1)
    experts = jnp.arange(N_EXPERTS, dtype=I32)
    onehot = block_eid[:, None] == experts[None, :]
    row_end = jnp.sum(jnp.where(onehot, (pstarts + counts)[None, :], 0), axis=1)
    block_valid = jnp.clip(row_end - block_row, 0, bm).astype(I32)
    active = block_valid > 0
    prev_eid = jnp.concatenate([jnp.full((1,), -1, I32), block_eid[:-1]])
    first = (active & (block_eid != prev_eid)).astype(I32)
    later = (experts[None, :] > experts[:, None]) & (counts[None, :] > 0)
    next_expert = jnp.min(jnp.where(later, experts[None, :], N_EXPERTS), axis=1)
    next_expert = jnp.where(next_expert == N_EXPERTS, -1, next_expert)
    block_next = jnp.sum(jnp.where(onehot, next_expert[None, :], 0), axis=1).astype(I32)
    slot = ((jnp.cumsum(first) - 1) % 2).astype(I32)
    return pos, (block_eid, block_valid, first, block_next, slot)


def _moe_rows(layer, hn2p, eidx, rank, cnt, wgu, bgu, wdn, bdn, after):
    n_tok, dh = hn2p.shape
    n_blocks = n_tok * TOP_K // EXPERT_ROWS + N_EXPERTS
    pos, tables = _route_tables(eidx, rank, cnt, n_blocks)
    xs = _sc_scatter_rows(hn2p, pos.reshape(TOP_K, n_tok // SC_CHUNK, SC_CHUNK), n_blocks * EXPERT_ROWS)
    ys = _expert_call(layer, tables, xs, wgu, bgu, wdn, bdn, after)
    yg = _sc_gather_rows(ys, pos.reshape(TOP_K * n_tok // SC_CHUNK, SC_CHUNK))
    return ys, yg.reshape(TOP_K, n_tok, dh)


def kernel(x, c, w_ada, b_ada, norm1_g, norm2_g, w_in, conv_w, conv_norm_g, lower_bounds, hgrn_norm_g, w_out,
           w_router, b_router, w_gu, b_gu, w_down, b_down, final_g):
    depth = w_ada.shape[0]
    bsz, seq, d = x.shape
    conv_ch = conv_w.shape[2]
    c_pad = jnp.zeros((SUBLANES, d), F32).at[:bsz].set(c)
    ada = _ada_call(c_pad, w_ada, b_ada)[:, :bsz].reshape(depth, bsz, 6, d)
    lb_all = _bounds_call(lower_bounds)

    ci = jnp.arange(conv_ch)
    gmean = jnp.where((ci[:, None] // CONV_GROUP) == (ci[None, :] // CONV_GROUP), 1.0 / CONV_GROUP, 0.0).astype(BF16)
    ti = jnp.arange(MIX_ROWS)
    upper = (ti[:, None] < ti[None, :]).astype(BF16)

    def mixer(l, b, xin, row, after, moe_input=None):
        win, wout = w_in[l].astype(BF16), w_out[l].astype(BF16)
        wr_hi = w_router[l].astype(BF16)
        wr_lo = (w_router[l] - wr_hi.astype(F32)).astype(BF16)
        wrt = jnp.zeros((d, 256), BF16).at[:, :N_EXPERTS].set(wr_hi).at[:, 128:128 + N_EXPERTS].set(wr_lo)
        br = jnp.broadcast_to(b_router[l][:, None], (N_EXPERTS, 128))
        return _mixer_call(xin, row, ada[l, b:b + 1], norm1_g[l][None], win, conv_w[l], conv_norm_g[l][None],
                           lb_all[l][None], hgrn_norm_g[l][None], wout, gmean, norm2_g[l][None], wrt, br, upper,
                           after, moe_input)

    def moe_input_of(l, b, mixed, yg):
        return (mixed[3].T, ada[l, b:b + 1, 5:6, :], yg)

    assert bsz == 2
    m0 = mixer(0, 0, x, 0, c_pad)
    m1 = mixer(0, 1, x, 1, m0[5])
    for l in range(depth):
        ys0, yg0 = _moe_rows(l, m0[1], m0[2], m0[4], m0[5], w_gu, b_gu, w_down, b_down, m1[5])
        ys1, yg1 = _moe_rows(l, m1[1], m1[2], m1[4], m1[5], w_gu, b_gu, w_down, b_down, ys0)
        if l + 1 < depth:
            n0 = mixer(l + 1, 0, m0[0], 0, ys1, moe_input_of(l, 0, m0, yg0))
            n1 = mixer(l + 1, 1, m1[0], 0, n0[5], moe_input_of(l, 1, m1, yg1))
            m0, m1 = n0, n1
    out = ys1
    for b, (mixed, yg) in enumerate(((m0, yg0), (m1, yg1))):
        gates_t, g2, _ = moe_input_of(depth - 1, b, mixed, yg)
        out = _final_call(mixed[0].reshape(seq, d), gates_t, g2.reshape(1, d), final_g[None], yg, out,
                          row=b, n_rows=bsz)
    return out
```

```python
import functools

import jax
import jax.numpy as jnp
from jax import lax
from jax.experimental import pallas as pl
from jax.experimental.pallas import tpu as pltpu
from jax.experimental.pallas import tpu_sc as plsc

F32 = jnp.float32
BF16 = jnp.bfloat16
I32 = jnp.int32
HIGHEST = lax.Precision.HIGHEST

RMS_EPS = 1e-6
N_EXPERTS = 32
TOP_K = 4
CONV_GROUP = 64
HEAD_DIM = 128
SWIGLU_LIMIT = 7.0
SWIGLU_ALPHA = 1.702
SUBLANES = 8
VMEM_LIMIT = 56 * 1024 * 1024

MIX_ROWS = 256
EXPERT_ROWS = 512
COMBINE_ROWS = 512
SC_CORES = 2
SC_SUBCORES = 16
SC_WORKERS = SC_CORES * SC_SUBCORES
SC_CHUNK = 64
ADA_COLS = 1536
WEIGHT_CAST_ROWS = 128


def _dot(a, b):
    return jnp.dot(a, b, preferred_element_type=F32)


def _dot_nt(a, b):
    return lax.dot_general(a, b, (((1,), (1,)), ((), ())), preferred_element_type=F32)


def _dot_tn(a, b):
    return lax.dot_general(a, b, (((0,), (0,)), ((), ())), preferred_element_type=F32)


def _sigmoid(x):
    return 1.0 / (1.0 + jnp.exp(-x))


def _rms(x):
    return x * lax.rsqrt(jnp.mean(x * x, axis=-1, keepdims=True) + RMS_EPS)


def _pack_bf16_pair(lo, hi):
    lo_bits = lax.bitcast_convert_type(lo.astype(BF16).astype(F32), I32)
    hi_bits = lax.bitcast_convert_type(hi.astype(BF16).astype(F32), I32)
    return lax.shift_right_logical(lo_bits, 16) | (hi_bits & jnp.int32(-65536))


def _unpack_bf16_pair(p):
    lo = lax.bitcast_convert_type(lax.shift_left(p, 16), F32)
    hi = lax.bitcast_convert_type(p & jnp.int32(-65536), F32)
    return lo, hi


def _ada_kernel(c_ref, w_ref, b_ref, o_ref):
    cv = c_ref[...]
    ca = cv * _sigmoid(cv)
    o_ref[0] = jnp.dot(ca, w_ref[0], precision=HIGHEST, preferred_element_type=F32) + b_ref[0]


def _ada_call(c_pad, w_ada, b_ada):
    depth, d, n6 = w_ada.shape
    rows = c_pad.shape[0]
    return pl.pallas_call(
        _ada_kernel,
        grid=(depth, n6 // ADA_COLS),
        in_specs=[
            pl.BlockSpec((rows, d), lambda l, j: (0, 0)),
            pl.BlockSpec((1, d, ADA_COLS), lambda l, j: (l, 0, j)),
            pl.BlockSpec((1, 1, ADA_COLS), lambda l, j: (l, 0, j)),
        ],
        out_specs=pl.BlockSpec((1, rows, ADA_COLS), lambda l, j: (l, 0, j)),
        out_shape=jax.ShapeDtypeStruct((depth, rows, n6), F32),
        compiler_params=pltpu.CompilerParams(
            dimension_semantics=("arbitrary", "arbitrary"), vmem_limit_bytes=VMEM_LIMIT),
        name="ada_proj",
    )(c_pad, w_ada, b_ada.reshape(depth, 1, n6))


def _bounds_kernel(lb_ref, o_ref):
    rows = [lb_ref[l:l + 1, :] for l in range(lb_ref.shape[0])]
    m = functools.reduce(jnp.maximum, rows)
    es = [jnp.exp(r - m) for r in rows]
    tot = functools.reduce(lambda a, b: a + b, es)
    cum = None
    first = None
    for l, e in enumerate(es):
        p = e / tot
        cum = p if cum is None else cum + p
        if first is None:
            first = cum
        o_ref[l:l + 1, :] = cum - first


def _bounds_call(lower_bounds):
    return pl.pallas_call(
        _bounds_kernel,
        out_shape=jax.ShapeDtypeStruct(lower_bounds.shape, F32),
        name="hgrn_bounds",
    )(lower_bounds.astype(F32))


def _small_level_exponents(g, row):
    t_rows, width = g.shape
    blocks = t_rows // SUBLANES

    def blockroll(x, k):
        return pltpu.roll(x.reshape(blocks, SUBLANES, width), k, 1).reshape(t_rows, width)

    a1 = g + blockroll(g, 1)
    a2 = a1 + blockroll(g, 2)
    a3 = a2 + blockroll(g, 3)
    b1 = blockroll(g, SUBLANES - 1)
    b2 = b1 + blockroll(g, SUBLANES - 2)
    b3 = b2 + blockroll(g, SUBLANES - 3)
    o = row & 7
    o2 = row & 3
    arg1 = jnp.where((row & 1) != 0, g, 0.0)
    arg2 = jnp.where(o2 == 3, a1, jnp.where(o2 == 2, g, jnp.where(o2 == 0, b1, 0.0)))
    up4 = jnp.where(o == 7, a3, jnp.where(o == 6, a2, jnp.where(o == 5, a1, g)))
    lo4 = jnp.where(o == 0, b3, jnp.where(o == 1, b2, jnp.where(o == 2, b1, 0.0)))
    arg4 = jnp.where(o >= 4, up4, lo4)
    return {1: arg1, 2: arg2, 4: arg4}


def _hgrn_head(q, k, v, g_cum, small_args, pair_masks, st_ref, h, row):
    t_rows = q.shape[0]
    scores = jnp.zeros((t_rows, t_rows), F32)
    half = t_rows // 2
    while half >= 1:
        grp = 2 * half
        upper = (row & half) != 0
        if half >= SUBLANES:
            g_mid = g_cum.reshape(t_rows // grp, grp, HEAD_DIM)[:, half - 1:half, :]
            g_mid = jnp.broadcast_to(g_mid, (t_rows // grp, grp, HEAD_DIM)).reshape(t_rows, HEAD_DIM)
            d = g_cum - g_mid
            arg = jnp.where(upper, d, -d)
        else:
            arg = small_args[half]
        xe = (jnp.where(upper, q, k) * jnp.exp(arg)).astype(BF16)
        scores = scores + jnp.where(pair_masks[half], _dot_nt(xe, xe), 0.0)
        half //= 2

    out = _dot(scores.astype(BF16), v.astype(BF16)) + jnp.sum(q * k, axis=-1, keepdims=True) * v

    st = st_ref[h]
    out = out + _dot_nt((q * jnp.exp(g_cum)).astype(BF16), st.astype(BF16))
    g_last = g_cum[t_rows - 1:t_rows, :]
    kd = (k * jnp.exp(g_last - g_cum)).astype(BF16)
    st_ref[h] = st * jnp.exp(g_last) + _dot_tn(v.astype(BF16), kd)
    return out


def _moe_residual(x, gate_cols, g2, yg_ref):
    t_rows, d = x.shape
    acc_lo = jnp.zeros((t_rows, d // 2), F32)
    acc_hi = jnp.zeros((t_rows, d // 2), F32)
    for kk in range(TOP_K):
        lo, hi = _unpack_bf16_pair(yg_ref[kk])
        gk = gate_cols[:, kk:kk + 1]
        acc_lo = acc_lo + gk * lo
        acc_hi = acc_hi + gk * hi
    return x + g2 * jnp.concatenate([acc_lo, acc_hi], axis=1)


def _mixer_kernel(*refs, conv_ch, hgrn_w, with_moe_input):
    if with_moe_input:
        pgate_ref, pg2_ref, yg_ref = refs[0:3]
        refs = refs[3:]
    (x_ref, ada_ref, n1g_ref, win_ref, convw_ref, cng_ref, lb_ref, hng_ref, wout_ref,
     gmean_ref, n2g_ref, wrt_ref, br_ref, upper_ref, after_ref,
     xo_ref, hn2_ref, eidx_ref, gate_ref, rank_ref, cnt_ref,
     st_ref, carry_ref, base_ref) = refs
    del after_ref
    i = pl.program_id(1)
    t_rows = x_ref.shape[1]
    heads = hgrn_w // HEAD_DIM

    @pl.when(i == 0)
    def _():
        st_ref[...] = jnp.zeros_like(st_ref)
        carry_ref[...] = jnp.zeros_like(carry_ref)
        base_ref[...] = jnp.zeros_like(base_ref)

    sh1, sc1, g1 = ada_ref[0, 0:1, :], ada_ref[0, 1:2, :], ada_ref[0, 2:3, :]
    sh2, sc2, g2 = ada_ref[0, 3:4, :], ada_ref[0, 4:5, :], ada_ref[0, 5:6, :]
    del g2

    x = x_ref[0]
    if with_moe_input:
        x = _moe_residual(x, pgate_ref[...], pg2_ref[0], yg_ref)
    hn = (_rms(x) * n1g_ref[...]) * (1.0 + sc1) + sh1
    proj = _dot(hn.astype(BF16), win_ref[...])
    c0 = conv_ch
    cb, cc, ch = proj[:, 0:c0], proj[:, c0:2 * c0], proj[:, 2 * c0:3 * c0]
    o0 = 3 * c0
    q_raw = proj[:, o0:o0 + hgrn_w]
    f_raw = proj[:, o0 + hgrn_w:o0 + 2 * hgrn_w]
    v_all = proj[:, o0 + 2 * hgrn_w:o0 + 3 * hgrn_w]
    og = proj[:, o0 + 3 * hgrn_w:o0 + 4 * hgrn_w]

    u = cc * ch
    carry = carry_ref[...]
    row8 = lax.broadcasted_iota(I32, (SUBLANES, c0), 0)
    u1 = pltpu.roll(u, 1, 0)
    u2 = pltpu.roll(u, 2, 0)
    u1 = jnp.concatenate([jnp.where(row8 < 1, pltpu.roll(carry, 1, 0), u1[0:SUBLANES]), u1[SUBLANES:]], axis=0)
    u2 = jnp.concatenate([jnp.where(row8 < 2, pltpu.roll(carry, 2, 0), u2[0:SUBLANES]), u2[SUBLANES:]], axis=0)
    carry_ref[...] = u[t_rows - SUBLANES:t_rows]
    yc = cb * (convw_ref[0:1, :] * u2 + convw_ref[1:2, :] * u1 + convw_ref[2:3, :] * u)
    sq = yc * yc
    sq_hi = sq.astype(BF16)
    sq_lo = (sq - sq_hi.astype(F32)).astype(BF16)
    gms = _dot(sq_hi, gmean_ref[...]) + _dot(sq_lo, gmean_ref[...])
    ycn = yc * lax.rsqrt(gms + RMS_EPS) * cng_ref[...]

    lb = lb_ref[...]
    log_lb = jnp.log(lb)
    log_1mlb = jnp.log1p(-lb)
    log_sig = jnp.minimum(f_raw, 0.0) - jnp.log(1.0 + jnp.exp(-jnp.abs(f_raw)))
    bb = log_1mlb + log_sig
    log_f = jnp.maximum(log_lb, bb) + jnp.log(1.0 + jnp.exp(-jnp.abs(log_lb - bb)))
    k_all = 1.0 - jnp.exp(log_f)
    q_all = q_raw * _sigmoid(q_raw)
    roww = lax.broadcasted_iota(I32, (t_rows, hgrn_w), 0)
    g_cum = log_f
    s = 1
    while s < t_rows:
        g_cum = g_cum + jnp.where(roww >= s, pltpu.roll(g_cum, s, 0), 0.0)
        s *= 2
    small = _small_level_exponents(log_f, roww)
    row = lax.broadcasted_iota(I32, (t_rows, HEAD_DIM), 0)
    rowsq = lax.broadcasted_iota(I32, (t_rows, t_rows), 0)
    col = lax.broadcasted_iota(I32, (t_rows, t_rows), 1)
    pair_masks = {}
    half = 1
    while half < t_rows:
        shift = half.bit_length() - 1
        t_blk = lax.shift_right_logical(rowsq, shift)
        s_blk = lax.shift_right_logical(col, shift)
        pair_masks[half] = (t_blk - s_blk == 1) & ((t_blk & 1) == 1)
        half *= 2
    outs = []
    for h in range(heads):
        sl = slice(h * HEAD_DIM, (h + 1) * HEAD_DIM)
        small_h = {lvl: a[:, sl] for lvl, a in small.items()}
        o = _hgrn_head(q_all[:, sl], k_all[:, sl], v_all[:, sl], g_cum[:, sl], small_h, pair_masks, st_ref, h,
                       row)
        outs.append(_rms(o))
    oh = jnp.concatenate(outs, axis=1) * hng_ref[...] * (og * _sigmoid(og))

    mix = _dot(ycn.astype(BF16), wout_ref[0:c0, :]) + _dot(oh.astype(BF16), wout_ref[c0:c0 + hgrn_w, :])
    xn = x + g1 * mix
    xo_ref[0] = xn

    hn2 = (_rms(xn) * n2g_ref[...]) * (1.0 + sc2) + sh2
    dh = hn2.shape[1] // 2
    hn2_ref[...] = _pack_bf16_pair(hn2[:, 0:dh], hn2[:, dh:2 * dh])
    h_hi = hn2.astype(BF16)
    h_lo = (hn2 - h_hi.astype(F32)).astype(BF16)
    prod = _dot(h_hi, wrt_ref[...])
    logits_tok = prod[:, 0:128] + prod[:, 128:256] + _dot(h_lo, wrt_ref[:, 0:128])
    logits = logits_tok.T[0:N_EXPERTS, :] + br_ref[:, 0:1]
    eio = lax.broadcasted_iota(I32, logits.shape, 0)
    order = jnp.zeros(logits.shape, F32)
    for e2 in range(N_EXPERTS):
        other = logits[e2:e2 + 1, :]
        beats = (other > logits) | ((other == logits) & (eio > e2))
        order = order + beats.astype(F32)
    eio_f = eio.astype(F32)
    vals, sels = [], []
    for kk in range(TOP_K):
        sel = order == float(kk)
        vals.append(jnp.sum(jnp.where(sel, logits, 0.0), axis=0, keepdims=True))
        sels.append(sel)
        eidx_ref[kk:kk + 1, :] = jnp.sum(jnp.where(sel, eio_f, 0.0), axis=0, keepdims=True).astype(I32)
    exps = [jnp.exp(vv - vals[0]) for vv in vals]
    tot = functools.reduce(lambda a, c: a + c, exps)
    member = functools.reduce(lambda a, c: a + c, [s_.astype(F32) for s_ in sels])
    before = _dot(member.astype(BF16), upper_ref[...])
    base = base_ref[:, 0:1]
    slot = before + base
    for kk in range(TOP_K):
        gate_ref[kk:kk + 1, :] = exps[kk] / tot
        rank_ref[kk:kk + 1, :] = jnp.sum(jnp.where(sels[kk], slot, 0.0), axis=0, keepdims=True).astype(I32)
    base_new = base + jnp.sum(member, axis=1, keepdims=True)
    base_ref[...] = jnp.broadcast_to(base_new, base_ref.shape)
    cnt_ref[...] = jnp.broadcast_to(base_new, cnt_ref.shape)


def _mixer_call(x, row, ada_l, n1g, win, convw, cng, lb, hng, wout, gmean, n2g, wrt, br, upper, after,
                moe_input=None):
    _, seq, d = x.shape
    bsz = 1
    t = MIX_ROWS
    n_tok = seq
    conv_ch = convw.shape[1]
    hgrn_w = lb.shape[1]
    heads = hgrn_w // HEAD_DIM
    steps = seq // t
    full = lambda a: pl.BlockSpec(a.shape, lambda b, i: (0,) * a.ndim)
    tokmap = lambda b, i: (0, b * steps + i)
    kern = functools.partial(_mixer_kernel, conv_ch=conv_ch, hgrn_w=hgrn_w, with_moe_input=moe_input is not None)
    moe_specs, moe_args = [], ()
    if moe_input is not None:
        moe_specs = [
            pl.BlockSpec((t, TOP_K), lambda b, i: (i, 0)),
            pl.BlockSpec((1, 1, d), lambda b, i: (0, 0, 0)),
            pl.BlockSpec((TOP_K, t, d // 2), lambda b, i: (0, i, 0)),
        ]
        moe_args = tuple(moe_input)
    return pl.pallas_call(
        kern,
        grid=(bsz, steps),
        in_specs=moe_specs + [
            pl.BlockSpec((1, t, d), lambda b, i: (row, i, 0)),
            pl.BlockSpec((1, 6, d), lambda b, i: (b, 0, 0)),
            full(n1g), full(win), full(convw), full(cng), full(lb), full(hng), full(wout),
            full(gmean), full(n2g), full(wrt), full(br), full(upper),
            pl.BlockSpec(memory_space=pl.ANY),
        ],
        out_specs=[
            pl.BlockSpec((1, t, d), lambda b, i: (b, i, 0)),
            pl.BlockSpec((t, d // 2), lambda b, i: (b * steps + i, 0)),
            pl.BlockSpec((TOP_K, t), tokmap),
            pl.BlockSpec((TOP_K, t), tokmap),
            pl.BlockSpec((TOP_K, t), tokmap),
            pl.BlockSpec((N_EXPERTS, 128), lambda b, i: (0, 0)),
        ],
        out_shape=[
            jax.ShapeDtypeStruct((bsz, seq, d), F32),
            jax.ShapeDtypeStruct((n_tok, d // 2), I32),
            jax.ShapeDtypeStruct((TOP_K, n_tok), I32),
            jax.ShapeDtypeStruct((TOP_K, n_tok), F32),
            jax.ShapeDtypeStruct((TOP_K, n_tok), I32),
            jax.ShapeDtypeStruct((N_EXPERTS, 128), F32),
        ],
        scratch_shapes=[
            pltpu.VMEM((heads, HEAD_DIM, HEAD_DIM), F32),
            pltpu.VMEM((SUBLANES, conv_ch), F32),
            pltpu.VMEM((N_EXPERTS, 128), F32),
        ],
        compiler_params=pltpu.CompilerParams(
            dimension_semantics=("arbitrary", "arbitrary"), vmem_limit_bytes=VMEM_LIMIT),
        name="token_mixer",
    )(*moe_args, x, ada_l, n1g, win, convw, cng, lb, hng, wout, gmean, n2g, wrt, br, upper, after)


def _sc_worker():
    return lax.axis_index("s") * SC_CORES + lax.axis_index("c")


def _sc_mesh():
    return plsc.VectorSubcoreMesh(core_axis_name="c", subcore_axis_name="s")


def _sc_scatter_rows(src, pos3d, p_rows):
    kk, n_idx_rows, ch = pos3d.shape
    d = src.shape[1]
    n_ch = n_idx_rows // SC_WORKERS
    assert n_idx_rows % SC_WORKERS == 0 and n_ch % 2 == 0

    @functools.partial(
        pl.kernel, mesh=_sc_mesh(), out_type=jax.ShapeDtypeStruct((p_rows, d), src.dtype),
        scratch_types=[pltpu.VMEM((kk, n_ch, ch), I32), pltpu.VMEM((2, ch, d), src.dtype),
                       pltpu.SemaphoreType.DMA((2,)), pltpu.SemaphoreType.DMA((2,))],
        name="sc_scatter_rows")
    def scatter_kernel(src_hbm, pos_hbm, out_hbm, idx_v, rows_v, lsem, ssem):
        wid = _sc_worker()
        for j in range(kk):
            pltpu.sync_copy(pos_hbm.at[j, pl.ds(wid * n_ch, n_ch)], idx_v.at[j])
        base = wid * (n_ch * ch)

        def load(c, slot):
            return pltpu.make_async_copy(src_hbm.at[pl.ds(base + c * ch, ch)], rows_v.at[slot], lsem.at[slot])

        def scatter(c, slot, j):
            return pltpu.make_async_copy(rows_v.at[slot], out_hbm.at[idx_v.at[j, c]], ssem.at[slot])

        load(0, 0).start()

        @pl.loop(0, n_ch, step=2)
        def _(c0):
            for slot in range(2):
                c = c0 + slot
                other = 1 - slot

                @pl.when(c >= 1)
                def _():
                    for j in range(kk):
                        scatter(c - 1, other, j).wait()

                @pl.when(c + 1 < n_ch)
                def _():
                    load(c + 1, other).start()

                load(c, slot).wait()
                for j in range(kk):
                    scatter(c, slot, j).start()

        for j in range(kk):
            scatter(n_ch - 1, (n_ch - 1) % 2, j).wait()

    return scatter_kernel(src, pos3d)


def _sc_gather_rows(table, idx2d):
    n_idx_rows, ch = idx2d.shape
    d = table.shape[1]
    n_ch = n_idx_rows // SC_WORKERS
    assert n_idx_rows % SC_WORKERS == 0 and n_ch % 2 == 0

    @functools.partial(
        pl.kernel, mesh=_sc_mesh(), out_type=jax.ShapeDtypeStruct((n_idx_rows * ch, d), table.dtype),
        scratch_types=[pltpu.VMEM((n_ch, ch), I32), pltpu.VMEM((2, ch, d), table.dtype),
                       pltpu.SemaphoreType.DMA((2,)), pltpu.SemaphoreType.DMA((2,))],
        name="sc_gather_rows")
    def gather_kernel(table_hbm, idx_hbm, out_hbm, idx_v, rows_v, gsem, wsem):
        wid = _sc_worker()
        pltpu.sync_copy(idx_hbm.at[pl.ds(wid * n_ch, n_ch)], idx_v)
        base = wid * (n_ch * ch)

        def gather(c, slot):
            return pltpu.make_async_copy(table_hbm.at[idx_v.at[c]], rows_v.at[slot], gsem.at[slot])

        def write(c, slot):
            return pltpu.make_async_copy(rows_v.at[slot], out_hbm.at[pl.ds(base + c * ch, ch)], wsem.at[slot])

        gather(0, 0).start()

        @pl.loop(0, n_ch, step=2)
        def _(c0):
            for slot in range(2):
                c = c0 + slot
                other = 1 - slot

                @pl.when(c >= 1)
                def _():
                    write(c - 1, other).wait()

                @pl.when(c + 1 < n_ch)
                def _():
                    gather(c + 1, other).start()

                gather(c, slot).wait()
                write(c, slot).start()

        write(n_ch - 1, (n_ch - 1) % 2).wait()

    return gather_kernel(table, idx2d)


def _expert_kernel(eid_ref, nvalid_ref, first_ref, next_ref, slot_ref, xs_ref, wgu_hbm, bgu_ref, wdn_hbm, bdn_ref,
                   after_ref, ys_ref, wgu_st, wdn_st, wgu_bf, wdn_bf, sem_gu, sem_dn, *, layer):
    del after_ref
    i = pl.program_id(0)
    nvalid = nvalid_ref[i]
    active = nvalid > 0

    def weight_copies(expert, slot):
        return (pltpu.make_async_copy(wgu_hbm.at[layer, expert], wgu_st.at[slot], sem_gu.at[slot]),
                pltpu.make_async_copy(wdn_hbm.at[layer, expert], wdn_st.at[slot], sem_dn.at[slot]))

    @pl.when(i == 0)
    def _():
        for cp in weight_copies(eid_ref[0], slot_ref[0]):
            cp.start()

    @pl.when(first_ref[i] == 1)
    def _():
        slot = slot_ref[i]
        for cp in weight_copies(eid_ref[i], slot):
            cp.wait()

        @pl.when(next_ref[i] >= 0)
        def _():
            for cp in weight_copies(next_ref[i], 1 - slot):
                cp.start()

        def cast_rows(r, carry):
            rows = pl.ds(pl.multiple_of(r * WEIGHT_CAST_ROWS, WEIGHT_CAST_ROWS), WEIGHT_CAST_ROWS)
            wgu_bf[rows, :] = wgu_st[slot, rows, :].astype(BF16)
            wdn_bf[rows, :] = wdn_st[slot, rows, :].astype(BF16)
            return carry
        lax.fori_loop(0, wgu_bf.shape[0] // WEIGHT_CAST_ROWS, cast_rows, 0)

    def mlp_rows(n_rows):
        packed = xs_ref[0:n_rows, :]
        keep = lax.broadcasted_iota(I32, packed.shape, 0) < nvalid
        lo, hi = _unpack_bf16_pair(jnp.where(keep, packed, 0))
        xb = jnp.concatenate([lo.astype(BF16), hi.astype(BF16)], axis=1)
        hgu = _dot(xb, wgu_bf[...]) + bgu_ref[0, 0]
        f = hgu.shape[1] // 2
        a = jnp.minimum(hgu[:, 0:f], SWIGLU_LIMIT)
        g = jnp.clip(hgu[:, f:2 * f], -SWIGLU_LIMIT, SWIGLU_LIMIT)
        act = a * _sigmoid(SWIGLU_ALPHA * a) * (g + 1.0)
        y = _dot(act.astype(BF16), wdn_bf[...]) + bdn_ref[0, 0]
        do = y.shape[1] // 2
        ys_ref[0:n_rows, :] = _pack_bf16_pair(y[:, 0:do], y[:, do:2 * do])

    full_rows = xs_ref.shape[0]
    lower = 0
    for n_rows in (full_rows // 4, full_rows // 2, full_rows):
        @pl.when((nvalid > lower) & (nvalid <= n_rows))
        def _(n_rows=n_rows):
            mlp_rows(n_rows)
            if n_rows < full_rows:
                ys_ref[n_rows:full_rows, :] = jnp.zeros((full_rows - n_rows, ys_ref.shape[1]), ys_ref.dtype)
        lower = n_rows

    @pl.when(jnp.logical_not(active))
    def _():
        ys_ref[...] = jnp.zeros_like(ys_ref)


def _expert_call(layer, tables, xs, wgu, bgu, wdn, bdn, after):
    p_rows, dh = xs.shape
    bm = EXPERT_ROWS
    depth, n_e, d, f2 = wgu.shape
    dout = wdn.shape[3]
    assert f2 // 2 == d and d % WEIGHT_CAST_ROWS == 0
    bias_map = lambda i, eid, *_: (layer, eid[i], 0, 0)
    grid_spec = pltpu.PrefetchScalarGridSpec(
        num_scalar_prefetch=len(tables),
        grid=(p_rows // bm,),
        in_specs=[
            pl.BlockSpec((bm, dh), lambda i, *_: (i, 0)),
            pl.BlockSpec(memory_space=pl.ANY),
            pl.BlockSpec((1, 1, 1, f2), bias_map),
            pl.BlockSpec(memory_space=pl.ANY),
            pl.BlockSpec((1, 1, 1, dout), bias_map),
            pl.BlockSpec(memory_space=pl.ANY),
        ],
        out_specs=pl.BlockSpec((bm, dout // 2), lambda i, *_: (i, 0)),
        scratch_shapes=[
            pltpu.VMEM((2, d, f2), F32), pltpu.VMEM((2, f2 // 2, dout), F32),
            pltpu.VMEM((d, f2), BF16), pltpu.VMEM((f2 // 2, dout), BF16),
            pltpu.SemaphoreType.DMA((2,)), pltpu.SemaphoreType.DMA((2,)),
        ],
    )
    return pl.pallas_call(
        functools.partial(_expert_kernel, layer=layer),
        grid_spec=grid_spec,
        out_shape=jax.ShapeDtypeStruct((p_rows, dout // 2), I32),
        compiler_params=pltpu.CompilerParams(
            dimension_semantics=("arbitrary",), vmem_limit_bytes=VMEM_LIMIT),
        name="moe_experts",
    )(*tables, xs, wgu, bgu.reshape(depth, n_e, 1, f2), wdn, bdn.reshape(depth, n_e, 1, dout), after)


def _final_kernel(x_ref, gate_ref, g2_ref, fg_ref, yg_ref, after_ref, o_ref):
    del after_ref
    out = _moe_residual(x_ref[...], gate_ref[...], g2_ref[...], yg_ref)
    o_ref[...] = _rms(out) * fg_ref[...]


def _final_call(x2d, gates_t, g2, final_g, yg, after, row, n_rows):
    n_tok, d = x2d.shape
    t = COMBINE_ROWS
    return pl.pallas_call(
        _final_kernel,
        grid=(n_tok // t,),
        in_specs=[
            pl.BlockSpec((t, d), lambda i: (i, 0)),
            pl.BlockSpec((t, TOP_K), lambda i: (i, 0)),
            pl.BlockSpec((1, d), lambda i: (0, 0)),
            pl.BlockSpec((1, d), lambda i: (0, 0)),
            pl.BlockSpec((TOP_K, t, d // 2), lambda i: (0, i, 0)),
            pl.BlockSpec(memory_space=pl.ANY),
        ],
        out_specs=pl.BlockSpec((None, t, d), lambda i: (row, i, 0)),
        out_shape=jax.ShapeDtypeStruct((n_rows, n_tok, d), F32),
        input_output_aliases={5: 0} if row > 0 else {},
        compiler_params=pltpu.CompilerParams(
            dimension_semantics=("arbitrary",), vmem_limit_bytes=VMEM_LIMIT),
        name="final_residual_norm",
    )(x2d, gates_t, g2, final_g, yg, after)


def _pos_kernel(pstart_ref, eidx_ref, rank_ref, pos_ref):
    e = eidx_ref[...]
    pos = rank_ref[...]
    for j in range(N_EXPERTS):
        pos = pos + jnp.where(e == j, pstart_ref[j], 0)
    pos_ref[...] = pos


def _pos_call(pstarts, eidx, rank):
    spec = pl.BlockSpec(eidx.shape, lambda i, ps: (0, 0))
    return pl.pallas_call(
        _pos_kernel,
        grid_spec=pltpu.PrefetchScalarGridSpec(num_scalar_prefetch=1, grid=(1,), in_specs=[spec, spec],
                                               out_specs=spec),
        out_shape=jax.ShapeDtypeStruct(eidx.shape, I32),
        name="moe_positions",
    )(pstarts, eidx, rank)


def _route_tables(eidx, rank, cnt, n_blocks):
    bm = EXPERT_ROWS
    counts = cnt[:, 0].astype(I32)
    padded = ((counts + bm - 1) // bm) * bm
    pends = jnp.cumsum(padded)
    pstarts = pends - padded
    pos = _pos_call(pstarts, eidx, rank)
    block_row = jnp.arange(n_blocks, dtype=I32) * bm
    block_eid = jnp.minimum(jnp.sum((pends[None, :] <= block_row[:, None]).astype(I32), axis=1), N_EXPERTS - 1)
    experts = jnp.arange(N_EXPERTS, dtype=I32)
    onehot = block_eid[:, None] == experts[None, :]
    row_end = jnp.sum(jnp.where(onehot, (pstarts + counts)[None, :], 0), axis=1)
    block_valid = jnp.clip(row_end - block_row, 0, bm).astype(I32)
    active = block_valid > 0
    prev_eid = jnp.concatenate([jnp.full((1,), -1, I32), block_eid[:-1]])
    first = (active & (block_eid != prev_eid)).astype(I32)
    later = (experts[None, :] > experts[:, None]) & (counts[None, :] > 0)
    next_expert = jnp.min(jnp.where(later, experts[None, :], N_EXPERTS), axis=1)
    next_expert = jnp.where(next_expert == N_EXPERTS, -1, next_expert)
    block_next = jnp.sum(jnp.where(onehot, next_expert[None, :], 0), axis=1).astype(I32)
    slot = ((jnp.cumsum(first) - 1) % 2).astype(I32)
    return pos, (block_eid, block_valid, first, block_next, slot)


def _moe_rows(layer, hn2p, eidx, rank, cnt, wgu, bgu, wdn, bdn, after):
    n_tok, dh = hn2p.shape
    n_blocks = n_tok * TOP_K // EXPERT_ROWS + N_EXPERTS
    pos, tables = _route_tables(eidx, rank, cnt, n_blocks)
    xs = _sc_scatter_rows(hn2p, pos.reshape(TOP_K, n_tok // SC_CHUNK, SC_CHUNK), n_blocks * EXPERT_ROWS)
    ys = _expert_call(layer, tables, xs, wgu, bgu, wdn, bdn, after)
    yg = _sc_gather_rows(ys, pos.reshape(TOP_K * n_tok // SC_CHUNK, SC_CHUNK))
    return ys, yg.reshape(TOP_K, n_tok, dh)


def kernel(x, c, w_ada, b_ada, norm1_g, norm2_g, w_in, conv_w, conv_norm_g, lower_bounds, hgrn_norm_g, w_out,
           w_router, b_router, w_gu, b_gu, w_down, b_down, final_g):
    depth = w_ada.shape[0]
    bsz, seq, d = x.shape
    conv_ch = conv_w.shape[2]
    c_pad = jnp.zeros((SUBLANES, d), F32).at[:bsz].set(c)
    ada = _ada_call(c_pad, w_ada, b_ada)[:, :bsz].reshape(depth, bsz, 6, d)
    lb_all = _bounds_call(lower_bounds)

    ci = jnp.arange(conv_ch)
    gmean = jnp.where((ci[:, None] // CONV_GROUP) == (ci[None, :] // CONV_GROUP), 1.0 / CONV_GROUP, 0.0).astype(BF16)
    ti = jnp.arange(MIX_ROWS)
    upper = (ti[:, None] < ti[None, :]).astype(BF16)

    def mixer(l, b, xin, row, after, moe_input=None):
        win, wout = w_in[l].astype(BF16), w_out[l].astype(BF16)
        wr_hi = w_router[l].astype(BF16)
        wr_lo = (w_router[l] - wr_hi.astype(F32)).astype(BF16)
        wrt = jnp.zeros((d, 256), BF16).at[:, :N_EXPERTS].set(wr_hi).at[:, 128:128 + N_EXPERTS].set(wr_lo)
        br = jnp.broadcast_to(b_router[l][:, None], (N_EXPERTS, 128))
        return _mixer_call(xin, row, ada[l, b:b + 1], norm1_g[l][None], win, conv_w[l], conv_norm_g[l][None],
                           lb_all[l][None], hgrn_norm_g[l][None], wout, gmean, norm2_g[l][None], wrt, br, upper,
                           after, moe_input)

    def moe_input_of(l, b, mixed, yg):
        return (mixed[3].T, ada[l, b:b + 1, 5:6, :], yg)

    assert bsz == 2
    m0 = mixer(0, 0, x, 0, c_pad)
    m1 = mixer(0, 1, x, 1, m0[5])
    for l in range(depth):
        ys0, yg0 = _moe_rows(l, m0[1], m0[2], m0[4], m0[5], w_gu, b_gu, w_down, b_down, m1[5])
        ys1, yg1 = _moe_rows(l, m1[1], m1[2], m1[4], m1[5], w_gu, b_gu, w_down, b_down, ys0)
        if l + 1 < depth:
            n0 = mixer(l + 1, 0, m0[0], 0, ys1, moe_input_of(l, 0, m0, yg0))
            n1 = mixer(l + 1, 1, m1[0], 0, n0[5], moe_input_of(l, 1, m1, yg1))
            m0, m1 = n0, n1
    out = ys1
    for b, (mixed, yg) in enumerate(((m0, yg0), (m1, yg1))):
        gates_t, g2, _ = moe_input_of(depth - 1, b, mixed, yg)
        out = _final_call(mixed[0].reshape(seq, d), gates_t, g2.reshape(1, d), final_g[None], yg, out,
                          row=b, n_rows=bsz)
    return out
```

```python
import functools

import jax
import jax.numpy as jnp
from jax import lax
from jax.experimental import pallas as pl
from jax.experimental.pallas import tpu as pltpu
from jax.experimental.pallas import tpu_sc as plsc

F32 = jnp.float32
BF16 = jnp.bfloat16
I32 = jnp.int32
HIGHEST = lax.Precision.HIGHEST

RMS_EPS = 1e-6
N_EXPERTS = 32
TOP_K = 4
CONV_GROUP = 64
HEAD_DIM = 128
SWIGLU_LIMIT = 7.0
SWIGLU_ALPHA = 1.702
SUBLANES = 8
VMEM_LIMIT = 56 * 1024 * 1024

MIX_ROWS = 256
EXPERT_ROWS = 512
COMBINE_ROWS = 512
SC_CORES = 2
SC_SUBCORES = 16
SC_WORKERS = SC_CORES * SC_SUBCORES
SC_CHUNK = 64
GATE_LANES = 128
ADA_COLS = 1536
WEIGHT_CAST_ROWS = 128


def _dot(a, b):
    return jnp.dot(a, b, preferred_element_type=F32)


def _dot_nt(a, b):
    return lax.dot_general(a, b, (((1,), (1,)), ((), ())), preferred_element_type=F32)


def _dot_tn(a, b):
    return lax.dot_general(a, b, (((0,), (0,)), ((), ())), preferred_element_type=F32)


def _sigmoid(x):
    return 1.0 / (1.0 + jnp.exp(-x))


def _rms(x):
    return x * lax.rsqrt(jnp.mean(x * x, axis=-1, keepdims=True) + RMS_EPS)


def _pack_bf16_pair(lo, hi):
    lo_bits = lax.bitcast_convert_type(lo.astype(BF16).astype(F32), I32)
    hi_bits = lax.bitcast_convert_type(hi.astype(BF16).astype(F32), I32)
    return lax.shift_right_logical(lo_bits, 16) | (hi_bits & jnp.int32(-65536))


def _unpack_bf16_pair(p):
    lo = lax.bitcast_convert_type(lax.shift_left(p, 16), F32)
    hi = lax.bitcast_convert_type(p & jnp.int32(-65536), F32)
    return lo, hi


def _ada_kernel(c_ref, w_ref, b_ref, o_ref):
    cv = c_ref[...]
    ca = cv * _sigmoid(cv)
    o_ref[0] = jnp.dot(ca, w_ref[0], precision=HIGHEST, preferred_element_type=F32) + b_ref[0]


def _ada_call(c_pad, w_ada, b_ada):
    depth, d, n6 = w_ada.shape
    rows = c_pad.shape[0]
    return pl.pallas_call(
        _ada_kernel,
        grid=(depth, n6 // ADA_COLS),
        in_specs=[
            pl.BlockSpec((rows, d), lambda l, j: (0, 0)),
            pl.BlockSpec((1, d, ADA_COLS), lambda l, j: (l, 0, j)),
            pl.BlockSpec((1, 1, ADA_COLS), lambda l, j: (l, 0, j)),
        ],
        out_specs=pl.BlockSpec((1, rows, ADA_COLS), lambda l, j: (l, 0, j)),
        out_shape=jax.ShapeDtypeStruct((depth, rows, n6), F32),
        compiler_params=pltpu.CompilerParams(
            dimension_semantics=("arbitrary", "arbitrary"), vmem_limit_bytes=VMEM_LIMIT),
        name="ada_proj",
    )(c_pad, w_ada, b_ada.reshape(depth, 1, n6))


def _bounds_kernel(lb_ref, o_ref):
    rows = [lb_ref[l:l + 1, :] for l in range(lb_ref.shape[0])]
    m = functools.reduce(jnp.maximum, rows)
    es = [jnp.exp(r - m) for r in rows]
    tot = functools.reduce(lambda a, b: a + b, es)
    cum = None
    first = None
    for l, e in enumerate(es):
        p = e / tot
        cum = p if cum is None else cum + p
        if first is None:
            first = cum
        o_ref[l:l + 1, :] = cum - first


def _bounds_call(lower_bounds):
    return pl.pallas_call(
        _bounds_kernel,
        out_shape=jax.ShapeDtypeStruct(lower_bounds.shape, F32),
        name="hgrn_bounds",
    )(lower_bounds.astype(F32))


def _small_level_exponents(g, row):
    t_rows, width = g.shape
    blocks = t_rows // SUBLANES

    def blockroll(x, k):
        return pltpu.roll(x.reshape(blocks, SUBLANES, width), k, 1).reshape(t_rows, width)

    a1 = g + blockroll(g, 1)
    a2 = a1 + blockroll(g, 2)
    a3 = a2 + blockroll(g, 3)
    b1 = blockroll(g, SUBLANES - 1)
    b2 = b1 + blockroll(g, SUBLANES - 2)
    b3 = b2 + blockroll(g, SUBLANES - 3)
    o = row & 7
    o2 = row & 3
    arg1 = jnp.where((row & 1) != 0, g, 0.0)
    arg2 = jnp.where(o2 == 3, a1, jnp.where(o2 == 2, g, jnp.where(o2 == 0, b1, 0.0)))
    up4 = jnp.where(o == 7, a3, jnp.where(o == 6, a2, jnp.where(o == 5, a1, g)))
    lo4 = jnp.where(o == 0, b3, jnp.where(o == 1, b2, jnp.where(o == 2, b1, 0.0)))
    arg4 = jnp.where(o >= 4, up4, lo4)
    return {1: arg1, 2: arg2, 4: arg4}


def _hgrn_head(q, k, v, g_cum, small_args, pair_masks, st_ref, h, row):
    t_rows = q.shape[0]
    scores = jnp.zeros((t_rows, t_rows), F32)
    half = t_rows // 2
    while half >= 1:
        grp = 2 * half
        upper = (row & half) != 0
        if half >= SUBLANES:
            g_mid = g_cum.reshape(t_rows // grp, grp, HEAD_DIM)[:, half - 1:half, :]
            g_mid = jnp.broadcast_to(g_mid, (t_rows // grp, grp, HEAD_DIM)).reshape(t_rows, HEAD_DIM)
            d = g_cum - g_mid
            arg = jnp.where(upper, d, -d)
        else:
            arg = small_args[half]
        xe = (jnp.where(upper, q, k) * jnp.exp(arg)).astype(BF16)
        scores = scores + jnp.where(pair_masks[half], _dot_nt(xe, xe), 0.0)
        half //= 2

    out = _dot(scores.astype(BF16), v.astype(BF16)) + jnp.sum(q * k, axis=-1, keepdims=True) * v

    st = st_ref[h]
    out = out + _dot_nt((q * jnp.exp(g_cum)).astype(BF16), st.astype(BF16))
    g_last = g_cum[t_rows - 1:t_rows, :]
    kd = (k * jnp.exp(g_last - g_cum)).astype(BF16)
    st_ref[h] = st * jnp.exp(g_last) + _dot_tn(v.astype(BF16), kd)
    return out


def _moe_residual(x, g2, yg_ref):
    t_rows, d = x.shape
    acc_lo = jnp.zeros((t_rows, d // 2), F32)
    acc_hi = jnp.zeros((t_rows, d // 2), F32)
    for kk in range(TOP_K):
        lo, hi = _unpack_bf16_pair(yg_ref[kk])
        acc_lo = acc_lo + lo
        acc_hi = acc_hi + hi
    return x + g2 * jnp.concatenate([acc_lo, acc_hi], axis=1)


def _mixer_kernel(*refs, conv_ch, hgrn_w, with_moe_input):
    if with_moe_input:
        pg2_ref, yg_ref = refs[0:2]
        refs = refs[2:]
    (x_ref, ada_ref, n1g_ref, win_ref, convw_ref, cng_ref, lb_ref, hng_ref, wout_ref,
     gmean_ref, n2g_ref, wrt_ref, br_ref, upper_ref, after_ref,
     xo_ref, hn2_ref, eidx_ref, gate_ref, rank_ref, cnt_ref,
     st_ref, carry_ref, base_ref) = refs
    del after_ref
    i = pl.program_id(1)
    t_rows = x_ref.shape[1]
    heads = hgrn_w // HEAD_DIM

    @pl.when(i == 0)
    def _():
        st_ref[...] = jnp.zeros_like(st_ref)
        carry_ref[...] = jnp.zeros_like(carry_ref)
        base_ref[...] = jnp.zeros_like(base_ref)

    sh1, sc1, g1 = ada_ref[0, 0:1, :], ada_ref[0, 1:2, :], ada_ref[0, 2:3, :]
    sh2, sc2, g2 = ada_ref[0, 3:4, :], ada_ref[0, 4:5, :], ada_ref[0, 5:6, :]
    del g2

    x = x_ref[0]
    if with_moe_input:
        x = _moe_residual(x, pg2_ref[0], yg_ref)
    hn = (_rms(x) * n1g_ref[...]) * (1.0 + sc1) + sh1
    proj = _dot(hn.astype(BF16), win_ref[...])
    c0 = conv_ch
    cb, cc, ch = proj[:, 0:c0], proj[:, c0:2 * c0], proj[:, 2 * c0:3 * c0]
    o0 = 3 * c0
    q_raw = proj[:, o0:o0 + hgrn_w]
    f_raw = proj[:, o0 + hgrn_w:o0 + 2 * hgrn_w]
    v_all = proj[:, o0 + 2 * hgrn_w:o0 + 3 * hgrn_w]
    og = proj[:, o0 + 3 * hgrn_w:o0 + 4 * hgrn_w]

    u = cc * ch
    carry = carry_ref[...]
    row8 = lax.broadcasted_iota(I32, (SUBLANES, c0), 0)
    u1 = pltpu.roll(u, 1, 0)
    u2 = pltpu.roll(u, 2, 0)
    u1 = jnp.concatenate([jnp.where(row8 < 1, pltpu.roll(carry, 1, 0), u1[0:SUBLANES]), u1[SUBLANES:]], axis=0)
    u2 = jnp.concatenate([jnp.where(row8 < 2, pltpu.roll(carry, 2, 0), u2[0:SUBLANES]), u2[SUBLANES:]], axis=0)
    carry_ref[...] = u[t_rows - SUBLANES:t_rows]
    yc = cb * (convw_ref[0:1, :] * u2 + convw_ref[1:2, :] * u1 + convw_ref[2:3, :] * u)
    sq = yc * yc
    sq_hi = sq.astype(BF16)
    sq_lo = (sq - sq_hi.astype(F32)).astype(BF16)
    gms = _dot(sq_hi, gmean_ref[...]) + _dot(sq_lo, gmean_ref[...])
    ycn = yc * lax.rsqrt(gms + RMS_EPS) * cng_ref[...]

    lb = lb_ref[...]
    log_lb = jnp.log(lb)
    log_1mlb = jnp.log1p(-lb)
    log_sig = jnp.minimum(f_raw, 0.0) - jnp.log(1.0 + jnp.exp(-jnp.abs(f_raw)))
    bb = log_1mlb + log_sig
    log_f = jnp.maximum(log_lb, bb) + jnp.log(1.0 + jnp.exp(-jnp.abs(log_lb - bb)))
    k_all = 1.0 - jnp.exp(log_f)
    q_all = q_raw * _sigmoid(q_raw)
    roww = lax.broadcasted_iota(I32, (t_rows, hgrn_w), 0)
    g_cum = log_f
    s = 1
    while s < t_rows:
        g_cum = g_cum + jnp.where(roww >= s, pltpu.roll(g_cum, s, 0), 0.0)
        s *= 2
    small = _small_level_exponents(log_f, roww)
    row = lax.broadcasted_iota(I32, (t_rows, HEAD_DIM), 0)
    rowsq = lax.broadcasted_iota(I32, (t_rows, t_rows), 0)
    col = lax.broadcasted_iota(I32, (t_rows, t_rows), 1)
    pair_masks = {}
    half = 1
    while half < t_rows:
        shift = half.bit_length() - 1
        t_blk = lax.shift_right_logical(rowsq, shift)
        s_blk = lax.shift_right_logical(col, shift)
        pair_masks[half] = (t_blk - s_blk == 1) & ((t_blk & 1) == 1)
        half *= 2
    outs = []
    for h in range(heads):
        sl = slice(h * HEAD_DIM, (h + 1) * HEAD_DIM)
        small_h = {lvl: a[:, sl] for lvl, a in small.items()}
        o = _hgrn_head(q_all[:, sl], k_all[:, sl], v_all[:, sl], g_cum[:, sl], small_h, pair_masks, st_ref, h,
                       row)
        outs.append(_rms(o))
    oh = jnp.concatenate(outs, axis=1) * hng_ref[...] * (og * _sigmoid(og))

    mix = _dot(ycn.astype(BF16), wout_ref[0:c0, :]) + _dot(oh.astype(BF16), wout_ref[c0:c0 + hgrn_w, :])
    xn = x + g1 * mix
    xo_ref[0] = xn

    hn2 = (_rms(xn) * n2g_ref[...]) * (1.0 + sc2) + sh2
    dh = hn2.shape[1] // 2
    hn2_ref[...] = _pack_bf16_pair(hn2[:, 0:dh], hn2[:, dh:2 * dh])
    h_hi = hn2.astype(BF16)
    h_lo = (hn2 - h_hi.astype(F32)).astype(BF16)
    prod = _dot(h_hi, wrt_ref[...])
    logits_tok = prod[:, 0:128] + prod[:, 128:256] + _dot(h_lo, wrt_ref[:, 0:128])
    logits = logits_tok.T[0:N_EXPERTS, :] + br_ref[:, 0:1]
    eio = lax.broadcasted_iota(I32, logits.shape, 0)
    order = jnp.zeros(logits.shape, F32)
    for e2 in range(N_EXPERTS):
        other = logits[e2:e2 + 1, :]
        beats = (other > logits) | ((other == logits) & (eio > e2))
        order = order + beats.astype(F32)
    eio_f = eio.astype(F32)
    vals, sels = [], []
    for kk in range(TOP_K):
        sel = order == float(kk)
        vals.append(jnp.sum(jnp.where(sel, logits, 0.0), axis=0, keepdims=True))
        sels.append(sel)
        eidx_ref[kk:kk + 1, :] = jnp.sum(jnp.where(sel, eio_f, 0.0), axis=0, keepdims=True).astype(I32)
    exps = [jnp.exp(vv - vals[0]) for vv in vals]
    tot = functools.reduce(lambda a, c: a + c, exps)
    member = functools.reduce(lambda a, c: a + c, [s_.astype(F32) for s_ in sels])
    before = _dot(member.astype(BF16), upper_ref[...])
    base = base_ref[:, 0:1]
    slot = before + base
    for kk in range(TOP_K):
        gate_ref[kk:kk + 1, :] = exps[kk] / tot
        rank_ref[kk:kk + 1, :] = jnp.sum(jnp.where(sels[kk], slot, 0.0), axis=0, keepdims=True).astype(I32)
    base_new = base + jnp.sum(member, axis=1, keepdims=True)
    base_ref[...] = jnp.broadcast_to(base_new, base_ref.shape)
    cnt_ref[...] = jnp.broadcast_to(base_new, cnt_ref.shape)


def _mixer_call(x, row, ada_l, n1g, win, convw, cng, lb, hng, wout, gmean, n2g, wrt, br, upper, after,
                moe_input=None):
    _, seq, d = x.shape
    bsz = 1
    t = MIX_ROWS
    n_tok = seq
    conv_ch = convw.shape[1]
    hgrn_w = lb.shape[1]
    heads = hgrn_w // HEAD_DIM
    steps = seq // t
    full = lambda a: pl.BlockSpec(a.shape, lambda b, i: (0,) * a.ndim)
    tokmap = lambda b, i: (0, b * steps + i)
    kern = functools.partial(_mixer_kernel, conv_ch=conv_ch, hgrn_w=hgrn_w, with_moe_input=moe_input is not None)
    moe_specs, moe_args = [], ()
    if moe_input is not None:
        moe_specs = [
            pl.BlockSpec((1, 1, d), lambda b, i: (0, 0, 0)),
            pl.BlockSpec((TOP_K, t, d // 2), lambda b, i: (0, i, 0)),
        ]
        moe_args = tuple(moe_input)
    return pl.pallas_call(
        kern,
        grid=(bsz, steps),
        in_specs=moe_specs + [
            pl.BlockSpec((1, t, d), lambda b, i: (row, i, 0)),
            pl.BlockSpec((1, 6, d), lambda b, i: (b, 0, 0)),
            full(n1g), full(win), full(convw), full(cng), full(lb), full(hng), full(wout),
            full(gmean), full(n2g), full(wrt), full(br), full(upper),
            pl.BlockSpec(memory_space=pl.ANY),
        ],
        out_specs=[
            pl.BlockSpec((1, t, d), lambda b, i: (b, i, 0)),
            pl.BlockSpec((t, d // 2), lambda b, i: (b * steps + i, 0)),
            pl.BlockSpec((TOP_K, t), tokmap),
            pl.BlockSpec((TOP_K, t), tokmap),
            pl.BlockSpec((TOP_K, t), tokmap),
            pl.BlockSpec((N_EXPERTS, 128), lambda b, i: (0, 0)),
        ],
        out_shape=[
            jax.ShapeDtypeStruct((bsz, seq, d), F32),
            jax.ShapeDtypeStruct((n_tok, d // 2), I32),
            jax.ShapeDtypeStruct((TOP_K, n_tok), I32),
            jax.ShapeDtypeStruct((TOP_K, n_tok), F32),
            jax.ShapeDtypeStruct((TOP_K, n_tok), I32),
            jax.ShapeDtypeStruct((N_EXPERTS, 128), F32),
        ],
        scratch_shapes=[
            pltpu.VMEM((heads, HEAD_DIM, HEAD_DIM), F32),
            pltpu.VMEM((SUBLANES, conv_ch), F32),
            pltpu.VMEM((N_EXPERTS, 128), F32),
        ],
        compiler_params=pltpu.CompilerParams(
            dimension_semantics=("arbitrary", "arbitrary"), vmem_limit_bytes=VMEM_LIMIT),
        name="token_mixer",
    )(*moe_args, x, ada_l, n1g, win, convw, cng, lb, hng, wout, gmean, n2g, wrt, br, upper, after)


def _sc_worker():
    return lax.axis_index("s") * SC_CORES + lax.axis_index("c")


def _sc_mesh():
    return plsc.VectorSubcoreMesh(core_axis_name="c", subcore_axis_name="s")


def _sc_scatter_rows(src, gate_rows, pos3d, p_rows):
    kk, n_idx_rows, ch = pos3d.shape
    d = src.shape[1]
    n_ch = n_idx_rows // SC_WORKERS
    assert n_idx_rows % SC_WORKERS == 0 and n_ch % 2 == 0

    @functools.partial(
        pl.kernel, mesh=_sc_mesh(),
        out_type=(jax.ShapeDtypeStruct((p_rows, d), src.dtype), jax.ShapeDtypeStruct((p_rows, GATE_LANES), F32)),
        scratch_types=[pltpu.VMEM((kk, n_ch, ch), I32), pltpu.VMEM((2, ch, d), src.dtype),
                       pltpu.VMEM((kk, ch, GATE_LANES), F32),
                       pltpu.SemaphoreType.DMA((2,)), pltpu.SemaphoreType.DMA((2,))],
        name="sc_scatter_rows")
    def scatter_kernel(src_hbm, gate_hbm, pos_hbm, out_hbm, gs_hbm, idx_v, rows_v, gate_v, lsem, ssem):
        wid = _sc_worker()
        for j in range(kk):
            pltpu.sync_copy(pos_hbm.at[j, pl.ds(wid * n_ch, n_ch)], idx_v.at[j])
        base = wid * (n_ch * ch)

        def load(c, slot):
            return pltpu.make_async_copy(src_hbm.at[pl.ds(base + c * ch, ch)], rows_v.at[slot], lsem.at[slot])

        def scatter(c, slot, j):
            return pltpu.make_async_copy(rows_v.at[slot], out_hbm.at[idx_v.at[j, c]], ssem.at[slot])

        load(0, 0).start()

        @pl.loop(0, n_ch, step=2)
        def _(c0):
            for slot in range(2):
                c = c0 + slot
                other = 1 - slot

                @pl.when(c >= 1)
                def _():
                    for j in range(kk):
                        scatter(c - 1, other, j).wait()

                @pl.when(c + 1 < n_ch)
                def _():
                    load(c + 1, other).start()

                load(c, slot).wait()
                for j in range(kk):
                    scatter(c, slot, j).start()
                for j in range(kk):
                    pltpu.sync_copy(gate_hbm.at[j, pl.ds(base + c * ch, ch)], gate_v.at[j])
                    pltpu.sync_copy(gate_v.at[j], gs_hbm.at[idx_v.at[j, c]])

        for j in range(kk):
            scatter(n_ch - 1, (n_ch - 1) % 2, j).wait()

    return scatter_kernel(src, gate_rows, pos3d)


def _sc_gather_rows(table, idx2d):
    n_idx_rows, ch = idx2d.shape
    d = table.shape[1]
    n_ch = n_idx_rows // SC_WORKERS
    assert n_idx_rows % SC_WORKERS == 0 and n_ch % 2 == 0

    @functools.partial(
        pl.kernel, mesh=_sc_mesh(), out_type=jax.ShapeDtypeStruct((n_idx_rows * ch, d), table.dtype),
        scratch_types=[pltpu.VMEM((n_ch, ch), I32), pltpu.VMEM((2, ch, d), table.dtype),
                       pltpu.SemaphoreType.DMA((2,)), pltpu.SemaphoreType.DMA((2,))],
        name="sc_gather_rows")
    def gather_kernel(table_hbm, idx_hbm, out_hbm, idx_v, rows_v, gsem, wsem):
        wid = _sc_worker()
        pltpu.sync_copy(idx_hbm.at[pl.ds(wid * n_ch, n_ch)], idx_v)
        base = wid * (n_ch * ch)

        def gather(c, slot):
            return pltpu.make_async_copy(table_hbm.at[idx_v.at[c]], rows_v.at[slot], gsem.at[slot])

        def write(c, slot):
            return pltpu.make_async_copy(rows_v.at[slot], out_hbm.at[pl.ds(base + c * ch, ch)], wsem.at[slot])

        gather(0, 0).start()

        @pl.loop(0, n_ch, step=2)
        def _(c0):
            for slot in range(2):
                c = c0 + slot
                other = 1 - slot

                @pl.when(c >= 1)
                def _():
                    write(c - 1, other).wait()

                @pl.when(c + 1 < n_ch)
                def _():
                    gather(c + 1, other).start()

                gather(c, slot).wait()
                write(c, slot).start()

        write(n_ch - 1, (n_ch - 1) % 2).wait()

    return gather_kernel(table, idx2d)


def _expert_kernel(eid_ref, nvalid_ref, first_ref, next_ref, slot_ref, xs_ref, gs_ref, wgu_hbm, bgu_ref, wdn_hbm, bdn_ref,
                   after_ref, ys_ref, wgu_st, wdn_st, wgu_bf, wdn_bf, sem_gu, sem_dn, *, layer):
    del after_ref
    i = pl.program_id(0)
    nvalid = nvalid_ref[i]
    active = nvalid > 0

    def weight_copies(expert, slot):
        return (pltpu.make_async_copy(wgu_hbm.at[layer, expert], wgu_st.at[slot], sem_gu.at[slot]),
                pltpu.make_async_copy(wdn_hbm.at[layer, expert], wdn_st.at[slot], sem_dn.at[slot]))

    @pl.when(i == 0)
    def _():
        for cp in weight_copies(eid_ref[0], slot_ref[0]):
            cp.start()

    @pl.when(first_ref[i] == 1)
    def _():
        slot = slot_ref[i]
        for cp in weight_copies(eid_ref[i], slot):
            cp.wait()

        @pl.when(next_ref[i] >= 0)
        def _():
            for cp in weight_copies(next_ref[i], 1 - slot):
                cp.start()

        def cast_rows(r, carry):
            rows = pl.ds(pl.multiple_of(r * WEIGHT_CAST_ROWS, WEIGHT_CAST_ROWS), WEIGHT_CAST_ROWS)
            wgu_bf[rows, :] = wgu_st[slot, rows, :].astype(BF16)
            wdn_bf[rows, :] = wdn_st[slot, rows, :].astype(BF16)
            return carry
        lax.fori_loop(0, wgu_bf.shape[0] // WEIGHT_CAST_ROWS, cast_rows, 0)

    def mlp_rows(n_rows):
        packed = xs_ref[0:n_rows, :]
        keep = lax.broadcasted_iota(I32, packed.shape, 0) < nvalid
        lo, hi = _unpack_bf16_pair(jnp.where(keep, packed, 0))
        xb = jnp.concatenate([lo.astype(BF16), hi.astype(BF16)], axis=1)
        hgu = _dot(xb, wgu_bf[...]) + bgu_ref[0, 0]
        f = hgu.shape[1] // 2
        a = jnp.minimum(hgu[:, 0:f], SWIGLU_LIMIT)
        g = jnp.clip(hgu[:, f:2 * f], -SWIGLU_LIMIT, SWIGLU_LIMIT)
        act = a * _sigmoid(SWIGLU_ALPHA * a) * (g + 1.0)
        gate = jnp.where(keep[:, 0:1], gs_ref[0:n_rows, 0:1], 0.0)
        y = (_dot(act.astype(BF16), wdn_bf[...]) + bdn_ref[0, 0]) * gate
        do = y.shape[1] // 2
        ys_ref[0:n_rows, :] = _pack_bf16_pair(y[:, 0:do], y[:, do:2 * do])

    full_rows = xs_ref.shape[0]
    lower = 0
    for n_rows in (full_rows // 4, full_rows // 2, full_rows):
        @pl.when((nvalid > lower) & (nvalid <= n_rows))
        def _(n_rows=n_rows):
            mlp_rows(n_rows)
            if n_rows < full_rows:
                ys_ref[n_rows:full_rows, :] = jnp.zeros((full_rows - n_rows, ys_ref.shape[1]), ys_ref.dtype)
        lower = n_rows

    @pl.when(jnp.logical_not(active))
    def _():
        ys_ref[...] = jnp.zeros_like(ys_ref)


def _expert_call(layer, tables, xs, gs, wgu, bgu, wdn, bdn, after):
    p_rows, dh = xs.shape
    bm = EXPERT_ROWS
    depth, n_e, d, f2 = wgu.shape
    dout = wdn.shape[3]
    assert f2 // 2 == d and d % WEIGHT_CAST_ROWS == 0
    bias_map = lambda i, eid, *_: (layer, eid[i], 0, 0)
    grid_spec = pltpu.PrefetchScalarGridSpec(
        num_scalar_prefetch=len(tables),
        grid=(p_rows // bm,),
        in_specs=[
            pl.BlockSpec((bm, dh), lambda i, *_: (i, 0)),
            pl.BlockSpec((bm, GATE_LANES), lambda i, *_: (i, 0)),
            pl.BlockSpec(memory_space=pl.ANY),
            pl.BlockSpec((1, 1, 1, f2), bias_map),
            pl.BlockSpec(memory_space=pl.ANY),
            pl.BlockSpec((1, 1, 1, dout), bias_map),
            pl.BlockSpec(memory_space=pl.ANY),
        ],
        out_specs=pl.BlockSpec((bm, dout // 2), lambda i, *_: (i, 0)),
        scratch_shapes=[
            pltpu.VMEM((2, d, f2), F32), pltpu.VMEM((2, f2 // 2, dout), F32),
            pltpu.VMEM((d, f2), BF16), pltpu.VMEM((f2 // 2, dout), BF16),
            pltpu.SemaphoreType.DMA((2,)), pltpu.SemaphoreType.DMA((2,)),
        ],
    )
    return pl.pallas_call(
        functools.partial(_expert_kernel, layer=layer),
        grid_spec=grid_spec,
        out_shape=jax.ShapeDtypeStruct((p_rows, dout // 2), I32),
        compiler_params=pltpu.CompilerParams(
            dimension_semantics=("arbitrary",), vmem_limit_bytes=VMEM_LIMIT),
        name="moe_experts",
    )(*tables, xs, gs, wgu, bgu.reshape(depth, n_e, 1, f2), wdn, bdn.reshape(depth, n_e, 1, dout), after)


def _final_kernel(x_ref, g2_ref, fg_ref, yg_ref, after_ref, o_ref):
    del after_ref
    out = _moe_residual(x_ref[...], g2_ref[...], yg_ref)
    o_ref[...] = _rms(out) * fg_ref[...]


def _final_call(x2d, g2, final_g, yg, after, row, n_rows):
    n_tok, d = x2d.shape
    t = COMBINE_ROWS
    return pl.pallas_call(
        _final_kernel,
        grid=(n_tok // t,),
        in_specs=[
            pl.BlockSpec((t, d), lambda i: (i, 0)),
            pl.BlockSpec((1, d), lambda i: (0, 0)),
            pl.BlockSpec((1, d), lambda i: (0, 0)),
            pl.BlockSpec((TOP_K, t, d // 2), lambda i: (0, i, 0)),
            pl.BlockSpec(memory_space=pl.ANY),
        ],
        out_specs=pl.BlockSpec((None, t, d), lambda i: (row, i, 0)),
        out_shape=jax.ShapeDtypeStruct((n_rows, n_tok, d), F32),
        input_output_aliases={4: 0} if row > 0 else {},
        compiler_params=pltpu.CompilerParams(
            dimension_semantics=("arbitrary",), vmem_limit_bytes=VMEM_LIMIT),
        name="final_residual_norm",
    )(x2d, g2, final_g, yg, after)


def _pos_kernel(pstart_ref, eidx_ref, rank_ref, pos_ref):
    e = eidx_ref[...]
    pos = rank_ref[...]
    for j in range(N_EXPERTS):
        pos = pos + jnp.where(e == j, pstart_ref[j], 0)
    pos_ref[...] = pos


def _pos_call(pstarts, eidx, rank):
    spec = pl.BlockSpec(eidx.shape, lambda i, ps: (0, 0))
    return pl.pallas_call(
        _pos_kernel,
        grid_spec=pltpu.PrefetchScalarGridSpec(num_scalar_prefetch=1, grid=(1,), in_specs=[spec, spec],
                                               out_specs=spec),
        out_shape=jax.ShapeDtypeStruct(eidx.shape, I32),
        name="moe_positions",
    )(pstarts, eidx, rank)


def _route_tables(eidx, rank, cnt, n_blocks):
    bm = EXPERT_ROWS
    counts = cnt[:, 0].astype(I32)
    padded = ((counts + bm - 1) // bm) * bm
    pends = jnp.cumsum(padded)
    pstarts = pends - padded
    pos = _pos_call(pstarts, eidx, rank)
    block_row = jnp.arange(n_blocks, dtype=I32) * bm
    block_eid = jnp.minimum(jnp.sum((pends[None, :] <= block_row[:, None]).astype(I32), axis=1), N_EXPERTS - 1)
    experts = jnp.arange(N_EXPERTS, dtype=I32)
    onehot = block_eid[:, None] == experts[None, :]
    row_end = jnp.sum(jnp.where(onehot, (pstarts + counts)[None, :], 0), axis=1)
    block_valid = jnp.clip(row_end - block_row, 0, bm).astype(I32)
    active = block_valid > 0
    prev_eid = jnp.concatenate([jnp.full((1,), -1, I32), block_eid[:-1]])
    first = (active & (block_eid != prev_eid)).astype(I32)
    later = (experts[None, :] > experts[:, None]) & (counts[None, :] > 0)
    next_expert = jnp.min(jnp.where(later, experts[None, :], N_EXPERTS), axis=1)
    next_expert = jnp.where(next_expert == N_EXPERTS, -1, next_expert)
    block_next = jnp.sum(jnp.where(onehot, next_expert[None, :], 0), axis=1).astype(I32)
    slot = ((jnp.cumsum(first) - 1) % 2).astype(I32)
    return pos, (block_eid, block_valid, first, block_next, slot)


def _moe_rows(layer, hn2p, eidx, gates, rank, cnt, wgu, bgu, wdn, bdn, after):
    n_tok, dh = hn2p.shape
    n_blocks = n_tok * TOP_K // EXPERT_ROWS + N_EXPERTS
    pos, tables = _route_tables(eidx, rank, cnt, n_blocks)
    gate_rows = jnp.broadcast_to(gates[:, :, None], (TOP_K, n_tok, GATE_LANES))
    xs, gs = _sc_scatter_rows(hn2p, gate_rows, pos.reshape(TOP_K, n_tok // SC_CHUNK, SC_CHUNK),
                              n_blocks * EXPERT_ROWS)
    ys = _expert_call(layer, tables, xs, gs, wgu, bgu, wdn, bdn, after)
    yg = _sc_gather_rows(ys, pos.reshape(TOP_K * n_tok // SC_CHUNK, SC_CHUNK))
    return ys, yg.reshape(TOP_K, n_tok, dh)


def kernel(x, c, w_ada, b_ada, norm1_g, norm2_g, w_in, conv_w, conv_norm_g, lower_bounds, hgrn_norm_g, w_out,
           w_router, b_router, w_gu, b_gu, w_down, b_down, final_g):
    depth = w_ada.shape[0]
    bsz, seq, d = x.shape
    conv_ch = conv_w.shape[2]
    c_pad = jnp.zeros((SUBLANES, d), F32).at[:bsz].set(c)
    ada = _ada_call(c_pad, w_ada, b_ada)[:, :bsz].reshape(depth, bsz, 6, d)
    lb_all = _bounds_call(lower_bounds)

    ci = jnp.arange(conv_ch)
    gmean = jnp.where((ci[:, None] // CONV_GROUP) == (ci[None, :] // CONV_GROUP), 1.0 / CONV_GROUP, 0.0).astype(BF16)
    ti = jnp.arange(MIX_ROWS)
    upper = (ti[:, None] < ti[None, :]).astype(BF16)

    def mixer(l, b, xin, row, after, moe_input=None):
        win, wout = w_in[l].astype(BF16), w_out[l].astype(BF16)
        wr_hi = w_router[l].astype(BF16)
        wr_lo = (w_router[l] - wr_hi.astype(F32)).astype(BF16)
        wrt = jnp.zeros((d, 256), BF16).at[:, :N_EXPERTS].set(wr_hi).at[:, 128:128 + N_EXPERTS].set(wr_lo)
        br = jnp.broadcast_to(b_router[l][:, None], (N_EXPERTS, 128))
        return _mixer_call(xin, row, ada[l, b:b + 1], norm1_g[l][None], win, conv_w[l], conv_norm_g[l][None],
                           lb_all[l][None], hgrn_norm_g[l][None], wout, gmean, norm2_g[l][None], wrt, br, upper,
                           after, moe_input)

    def moe_input_of(l, b, yg):
        return (ada[l, b:b + 1, 5:6, :], yg)

    assert bsz == 2
    m0 = mixer(0, 0, x, 0, c_pad)
    m1 = mixer(0, 1, x, 1, m0[5])
    for l in range(depth):
        ys0, yg0 = _moe_rows(l, m0[1], m0[2], m0[3], m0[4], m0[5], w_gu, b_gu, w_down, b_down, m1[5])
        ys1, yg1 = _moe_rows(l, m1[1], m1[2], m1[3], m1[4], m1[5], w_gu, b_gu, w_down, b_down, ys0)
        if l + 1 < depth:
            n0 = mixer(l + 1, 0, m0[0], 0, ys1, moe_input_of(l, 0, yg0))
            n1 = mixer(l + 1, 1, m1[0], 0, n0[5], moe_input_of(l, 1, yg1))
            m0, m1 = n0, n1
    out = ys1
    for b, (mixed, yg) in enumerate(((m0, yg0), (m1, yg1))):
        g2, _ = moe_input_of(depth - 1, b, yg)
        out = _final_call(mixed[0].reshape(seq, d), g2.reshape(1, d), final_g[None], yg, out, row=b, n_rows=bsz)
    return out
```

```python
import functools

import jax
import jax.numpy as jnp
from jax import lax
from jax.experimental import pallas as pl
from jax.experimental.pallas import tpu as pltpu
from jax.experimental.pallas import tpu_sc as plsc

F32 = jnp.float32
BF16 = jnp.bfloat16
I32 = jnp.int32
HIGHEST = lax.Precision.HIGHEST

RMS_EPS = 1e-6
N_EXPERTS = 32
TOP_K = 4
CONV_GROUP = 64
HEAD_DIM = 128
SWIGLU_LIMIT = 7.0
SWIGLU_ALPHA = 1.702
SUBLANES = 8
VMEM_LIMIT = 56 * 1024 * 1024

MIX_ROWS = 256
EXPERT_ROWS = 512
COMBINE_ROWS = 512
SC_CORES = 2
SC_SUBCORES = 16
SC_WORKERS = SC_CORES * SC_SUBCORES
SC_CHUNK = 64
GATE_LANES = 128
ADA_COLS = 1536
WEIGHT_CAST_ROWS = 128


def _dot(a, b):
    return jnp.dot(a, b, preferred_element_type=F32)


def _dot_nt(a, b):
    return lax.dot_general(a, b, (((1,), (1,)), ((), ())), preferred_element_type=F32)


def _dot_tn(a, b):
    return lax.dot_general(a, b, (((0,), (0,)), ((), ())), preferred_element_type=F32)


def _sigmoid(x):
    return 1.0 / (1.0 + jnp.exp(-x))


def _rms(x):
    return x * lax.rsqrt(jnp.mean(x * x, axis=-1, keepdims=True) + RMS_EPS)


def _pack_bf16_pair(lo, hi):
    lo_bits = lax.bitcast_convert_type(lo.astype(BF16).astype(F32), I32)
    hi_bits = lax.bitcast_convert_type(hi.astype(BF16).astype(F32), I32)
    return lax.shift_right_logical(lo_bits, 16) | (hi_bits & jnp.int32(-65536))


def _unpack_bf16_pair(p):
    lo = lax.bitcast_convert_type(lax.shift_left(p, 16), F32)
    hi = lax.bitcast_convert_type(p & jnp.int32(-65536), F32)
    return lo, hi


def _ada_kernel(c_ref, w_ref, b_ref, o_ref):
    cv = c_ref[...]
    ca = cv * _sigmoid(cv)
    o_ref[0] = jnp.dot(ca, w_ref[0], precision=HIGHEST, preferred_element_type=F32) + b_ref[0]


def _ada_call(c_pad, w_ada, b_ada):
    depth, d, n6 = w_ada.shape
    rows = c_pad.shape[0]
    return pl.pallas_call(
        _ada_kernel,
        grid=(depth, n6 // ADA_COLS),
        in_specs=[
            pl.BlockSpec((rows, d), lambda l, j: (0, 0)),
            pl.BlockSpec((1, d, ADA_COLS), lambda l, j: (l, 0, j)),
            pl.BlockSpec((1, 1, ADA_COLS), lambda l, j: (l, 0, j)),
        ],
        out_specs=pl.BlockSpec((1, rows, ADA_COLS), lambda l, j: (l, 0, j)),
        out_shape=jax.ShapeDtypeStruct((depth, rows, n6), F32),
        compiler_params=pltpu.CompilerParams(
            dimension_semantics=("arbitrary", "arbitrary"), vmem_limit_bytes=VMEM_LIMIT),
        name="ada_proj",
    )(c_pad, w_ada, b_ada.reshape(depth, 1, n6))


def _bounds_kernel(lb_ref, o_ref):
    rows = [lb_ref[l:l + 1, :] for l in range(lb_ref.shape[0])]
    m = functools.reduce(jnp.maximum, rows)
    es = [jnp.exp(r - m) for r in rows]
    tot = functools.reduce(lambda a, b: a + b, es)
    cum = None
    first = None
    for l, e in enumerate(es):
        p = e / tot
        cum = p if cum is None else cum + p
        if first is None:
            first = cum
        o_ref[l:l + 1, :] = cum - first


def _bounds_call(lower_bounds):
    return pl.pallas_call(
        _bounds_kernel,
        out_shape=jax.ShapeDtypeStruct(lower_bounds.shape, F32),
        name="hgrn_bounds",
    )(lower_bounds.astype(F32))


def _small_level_exponents(g, row):
    t_rows, width = g.shape
    blocks = t_rows // SUBLANES

    def blockroll(x, k):
        return pltpu.roll(x.reshape(blocks, SUBLANES, width), k, 1).reshape(t_rows, width)

    a1 = g + blockroll(g, 1)
    a2 = a1 + blockroll(g, 2)
    a3 = a2 + blockroll(g, 3)
    b1 = blockroll(g, SUBLANES - 1)
    b2 = b1 + blockroll(g, SUBLANES - 2)
    b3 = b2 + blockroll(g, SUBLANES - 3)
    o = row & 7
    o2 = row & 3
    arg1 = jnp.where((row & 1) != 0, g, 0.0)
    arg2 = jnp.where(o2 == 3, a1, jnp.where(o2 == 2, g, jnp.where(o2 == 0, b1, 0.0)))
    up4 = jnp.where(o == 7, a3, jnp.where(o == 6, a2, jnp.where(o == 5, a1, g)))
    lo4 = jnp.where(o == 0, b3, jnp.where(o == 1, b2, jnp.where(o == 2, b1, 0.0)))
    arg4 = jnp.where(o >= 4, up4, lo4)
    return {1: arg1, 2: arg2, 4: arg4}


def _hgrn_head(q, k, v, g_cum, small_args, pair_masks, st_ref, h, row):
    t_rows = q.shape[0]
    scores = jnp.zeros((t_rows, t_rows), F32)
    half = t_rows // 2
    while half >= 1:
        grp = 2 * half
        upper = (row & half) != 0
        if half >= SUBLANES:
            g_mid = g_cum.reshape(t_rows // grp, grp, HEAD_DIM)[:, half - 1:half, :]
            g_mid = jnp.broadcast_to(g_mid, (t_rows // grp, grp, HEAD_DIM)).reshape(t_rows, HEAD_DIM)
            d = g_cum - g_mid
            arg = jnp.where(upper, d, -d)
        else:
            arg = small_args[half]
        xe = (jnp.where(upper, q, k) * jnp.exp(arg)).astype(BF16)
        scores = scores + jnp.where(pair_masks[half], _dot_nt(xe, xe), 0.0)
        half //= 2

    out = _dot(scores.astype(BF16), v.astype(BF16)) + jnp.sum(q * k, axis=-1, keepdims=True) * v

    st = st_ref[h]
    out = out + _dot_nt((q * jnp.exp(g_cum)).astype(BF16), st.astype(BF16))
    g_last = g_cum[t_rows - 1:t_rows, :]
    kd = (k * jnp.exp(g_last - g_cum)).astype(BF16)
    st_ref[h] = st * jnp.exp(g_last) + _dot_tn(v.astype(BF16), kd)
    return out


def _moe_residual(x, g2, yg_ref):
    t_rows, d = x.shape
    acc_lo = jnp.zeros((t_rows, d // 2), F32)
    acc_hi = jnp.zeros((t_rows, d // 2), F32)
    for kk in range(TOP_K):
        lo, hi = _unpack_bf16_pair(yg_ref[kk])
        acc_lo = acc_lo + lo
        acc_hi = acc_hi + hi
    return x + g2 * jnp.concatenate([acc_lo, acc_hi], axis=1)


def _mixer_kernel(*refs, conv_ch, hgrn_w, with_moe_input):
    if with_moe_input:
        pg2_ref, yg_ref = refs[0:2]
        refs = refs[2:]
    (x_ref, ada_ref, n1g_ref, win_ref, convw_ref, cng_ref, lb_ref, hng_ref, wout_ref,
     gmean_ref, n2g_ref, wrt_ref, br_ref, upper_ref, after_ref,
     xo_ref, hn2_ref, eidx_ref, gate_ref, rank_ref, cnt_ref,
     st_ref, carry_ref, base_ref) = refs
    del after_ref
    i = pl.program_id(1)
    t_rows = x_ref.shape[1]
    heads = hgrn_w // HEAD_DIM

    @pl.when(i == 0)
    def _():
        st_ref[...] = jnp.zeros_like(st_ref)
        carry_ref[...] = jnp.zeros_like(carry_ref)
        base_ref[...] = jnp.zeros_like(base_ref)

    sh1, sc1, g1 = ada_ref[0, 0:1, :], ada_ref[0, 1:2, :], ada_ref[0, 2:3, :]
    sh2, sc2, g2 = ada_ref[0, 3:4, :], ada_ref[0, 4:5, :], ada_ref[0, 5:6, :]
    del g2

    x = x_ref[0]
    if with_moe_input:
        x = _moe_residual(x, pg2_ref[0], yg_ref)
    hn = (_rms(x) * n1g_ref[...]) * (1.0 + sc1) + sh1
    proj = _dot(hn.astype(BF16), win_ref[...])
    c0 = conv_ch
    cb, cc, ch = proj[:, 0:c0], proj[:, c0:2 * c0], proj[:, 2 * c0:3 * c0]
    o0 = 3 * c0
    q_raw = proj[:, o0:o0 + hgrn_w]
    f_raw = proj[:, o0 + hgrn_w:o0 + 2 * hgrn_w]
    v_all = proj[:, o0 + 2 * hgrn_w:o0 + 3 * hgrn_w]
    og = proj[:, o0 + 3 * hgrn_w:o0 + 4 * hgrn_w]

    u = cc * ch
    carry = carry_ref[...]
    row8 = lax.broadcasted_iota(I32, (SUBLANES, c0), 0)
    u1 = pltpu.roll(u, 1, 0)
    u2 = pltpu.roll(u, 2, 0)
    u1 = jnp.concatenate([jnp.where(row8 < 1, pltpu.roll(carry, 1, 0), u1[0:SUBLANES]), u1[SUBLANES:]], axis=0)
    u2 = jnp.concatenate([jnp.where(row8 < 2, pltpu.roll(carry, 2, 0), u2[0:SUBLANES]), u2[SUBLANES:]], axis=0)
    carry_ref[...] = u[t_rows - SUBLANES:t_rows]
    yc = cb * (convw_ref[0:1, :] * u2 + convw_ref[1:2, :] * u1 + convw_ref[2:3, :] * u)
    sq = yc * yc
    sq_hi = sq.astype(BF16)
    sq_lo = (sq - sq_hi.astype(F32)).astype(BF16)
    gms = _dot(sq_hi, gmean_ref[...]) + _dot(sq_lo, gmean_ref[...])
    ycn = yc * lax.rsqrt(gms + RMS_EPS) * cng_ref[...]

    lb = lb_ref[...]
    log_lb = jnp.log(lb)
    log_1mlb = jnp.log1p(-lb)
    log_sig = jnp.minimum(f_raw, 0.0) - jnp.log(1.0 + jnp.exp(-jnp.abs(f_raw)))
    bb = log_1mlb + log_sig
    log_f = jnp.maximum(log_lb, bb) + jnp.log(1.0 + jnp.exp(-jnp.abs(log_lb - bb)))
    k_all = 1.0 - jnp.exp(log_f)
    q_all = q_raw * _sigmoid(q_raw)
    roww = lax.broadcasted_iota(I32, (t_rows, hgrn_w), 0)
    g_cum = log_f
    s = 1
    while s < t_rows:
        g_cum = g_cum + jnp.where(roww >= s, pltpu.roll(g_cum, s, 0), 0.0)
        s *= 2
    small = _small_level_exponents(log_f, roww)
    row = lax.broadcasted_iota(I32, (t_rows, HEAD_DIM), 0)
    rowsq = lax.broadcasted_iota(I32, (t_rows, t_rows), 0)
    col = lax.broadcasted_iota(I32, (t_rows, t_rows), 1)
    pair_masks = {}
    half = 1
    while half < t_rows:
        shift = half.bit_length() - 1
        t_blk = lax.shift_right_logical(rowsq, shift)
        s_blk = lax.shift_right_logical(col, shift)
        pair_masks[half] = (t_blk - s_blk == 1) & ((t_blk & 1) == 1)
        half *= 2
    outs = []
    for h in range(heads):
        sl = slice(h * HEAD_DIM, (h + 1) * HEAD_DIM)
        small_h = {lvl: a[:, sl] for lvl, a in small.items()}
        o = _hgrn_head(q_all[:, sl], k_all[:, sl], v_all[:, sl], g_cum[:, sl], small_h, pair_masks, st_ref, h,
                       row)
        outs.append(_rms(o))
    oh = jnp.concatenate(outs, axis=1) * hng_ref[...] * (og * _sigmoid(og))

    mix = _dot(ycn.astype(BF16), wout_ref[0:c0, :]) + _dot(oh.astype(BF16), wout_ref[c0:c0 + hgrn_w, :])
    xn = x + g1 * mix
    xo_ref[0] = xn

    hn2 = (_rms(xn) * n2g_ref[...]) * (1.0 + sc2) + sh2
    dh = hn2.shape[1] // 2
    hn2_ref[...] = _pack_bf16_pair(hn2[:, 0:dh], hn2[:, dh:2 * dh])
    h_hi = hn2.astype(BF16)
    h_lo = (hn2 - h_hi.astype(F32)).astype(BF16)
    prod = _dot(h_hi, wrt_ref[...])
    logits_tok = prod[:, 0:128] + prod[:, 128:256] + _dot(h_lo, wrt_ref[:, 0:128])
    logits = logits_tok.T[0:N_EXPERTS, :] + br_ref[:, 0:1]
    eio = lax.broadcasted_iota(I32, logits.shape, 0)
    order = jnp.zeros(logits.shape, F32)
    for e2 in range(N_EXPERTS):
        other = logits[e2:e2 + 1, :]
        beats = (other > logits) | ((other == logits) & (eio > e2))
        order = order + beats.astype(F32)
    eio_f = eio.astype(F32)
    vals, sels = [], []
    for kk in range(TOP_K):
        sel = order == float(kk)
        vals.append(jnp.sum(jnp.where(sel, logits, 0.0), axis=0, keepdims=True))
        sels.append(sel)
        eidx_ref[kk:kk + 1, :] = jnp.sum(jnp.where(sel, eio_f, 0.0), axis=0, keepdims=True).astype(I32)
    exps = [jnp.exp(vv - vals[0]) for vv in vals]
    tot = functools.reduce(lambda a, c: a + c, exps)
    member = functools.reduce(lambda a, c: a + c, [s_.astype(F32) for s_ in sels])
    before = _dot(member.astype(BF16), upper_ref[...])
    base = base_ref[:, 0:1]
    slot = before + base
    sub8 = lax.broadcasted_iota(I32, (SUBLANES, t_rows), 0)
    gates8 = jnp.zeros((SUBLANES, t_rows), F32)
    for kk in range(TOP_K):
        gates8 = jnp.where(sub8 == kk, exps[kk] / tot, gates8)
        rank_ref[kk:kk + 1, :] = jnp.sum(jnp.where(sels[kk], slot, 0.0), axis=0, keepdims=True).astype(I32)
    gates_tok = jnp.concatenate([gates8, jnp.zeros((GATE_LANES - SUBLANES, t_rows), F32)], axis=0).T
    for kk in range(TOP_K):
        gate_ref[kk] = jnp.broadcast_to(gates_tok[:, kk:kk + 1], (t_rows, GATE_LANES))
    base_new = base + jnp.sum(member, axis=1, keepdims=True)
    base_ref[...] = jnp.broadcast_to(base_new, base_ref.shape)
    cnt_ref[...] = jnp.broadcast_to(base_new, cnt_ref.shape)


def _mixer_call(x, row, ada_l, n1g, win, convw, cng, lb, hng, wout, gmean, n2g, wrt, br, upper, after,
                moe_input=None):
    _, seq, d = x.shape
    bsz = 1
    t = MIX_ROWS
    n_tok = seq
    conv_ch = convw.shape[1]
    hgrn_w = lb.shape[1]
    heads = hgrn_w // HEAD_DIM
    steps = seq // t
    full = lambda a: pl.BlockSpec(a.shape, lambda b, i: (0,) * a.ndim)
    tokmap = lambda b, i: (0, b * steps + i)
    kern = functools.partial(_mixer_kernel, conv_ch=conv_ch, hgrn_w=hgrn_w, with_moe_input=moe_input is not None)
    moe_specs, moe_args = [], ()
    if moe_input is not None:
        moe_specs = [
            pl.BlockSpec((1, 1, d), lambda b, i: (0, 0, 0)),
            pl.BlockSpec((TOP_K, t, d // 2), lambda b, i: (0, i, 0)),
        ]
        moe_args = tuple(moe_input)
    return pl.pallas_call(
        kern,
        grid=(bsz, steps),
        in_specs=moe_specs + [
            pl.BlockSpec((1, t, d), lambda b, i: (row, i, 0)),
            pl.BlockSpec((1, 6, d), lambda b, i: (b, 0, 0)),
            full(n1g), full(win), full(convw), full(cng), full(lb), full(hng), full(wout),
            full(gmean), full(n2g), full(wrt), full(br), full(upper),
            pl.BlockSpec(memory_space=pl.ANY),
        ],
        out_specs=[
            pl.BlockSpec((1, t, d), lambda b, i: (b, i, 0)),
            pl.BlockSpec((t, d // 2), lambda b, i: (b * steps + i, 0)),
            pl.BlockSpec((TOP_K, t), tokmap),
            pl.BlockSpec((TOP_K, t, GATE_LANES), lambda b, i: (0, b * steps + i, 0)),
            pl.BlockSpec((TOP_K, t), tokmap),
            pl.BlockSpec((N_EXPERTS, 128), lambda b, i: (0, 0)),
        ],
        out_shape=[
            jax.ShapeDtypeStruct((bsz, seq, d), F32),
            jax.ShapeDtypeStruct((n_tok, d // 2), I32),
            jax.ShapeDtypeStruct((TOP_K, n_tok), I32),
            jax.ShapeDtypeStruct((TOP_K, n_tok, GATE_LANES), F32),
            jax.ShapeDtypeStruct((TOP_K, n_tok), I32),
            jax.ShapeDtypeStruct((N_EXPERTS, 128), F32),
        ],
        scratch_shapes=[
            pltpu.VMEM((heads, HEAD_DIM, HEAD_DIM), F32),
            pltpu.VMEM((SUBLANES, conv_ch), F32),
            pltpu.VMEM((N_EXPERTS, 128), F32),
        ],
        compiler_params=pltpu.CompilerParams(
            dimension_semantics=("arbitrary", "arbitrary"), vmem_limit_bytes=VMEM_LIMIT),
        name="token_mixer",
    )(*moe_args, x, ada_l, n1g, win, convw, cng, lb, hng, wout, gmean, n2g, wrt, br, upper, after)


def _sc_worker():
    return lax.axis_index("s") * SC_CORES + lax.axis_index("c")


def _sc_mesh():
    return plsc.VectorSubcoreMesh(core_axis_name="c", subcore_axis_name="s")


def _sc_scatter_rows(src, gate_rows, pos3d, p_rows):
    kk, n_idx_rows, ch = pos3d.shape
    d = src.shape[1]
    n_ch = n_idx_rows // SC_WORKERS
    assert n_idx_rows % SC_WORKERS == 0 and n_ch % 2 == 0

    @functools.partial(
        pl.kernel, mesh=_sc_mesh(),
        out_type=(jax.ShapeDtypeStruct((p_rows, d), src.dtype), jax.ShapeDtypeStruct((p_rows, GATE_LANES), F32)),
        scratch_types=[pltpu.VMEM((kk, n_ch, ch), I32), pltpu.VMEM((2, ch, d), src.dtype),
                       pltpu.VMEM((kk, ch, GATE_LANES), F32),
                       pltpu.SemaphoreType.DMA((2,)), pltpu.SemaphoreType.DMA((2,))],
        name="sc_scatter_rows")
    def scatter_kernel(src_hbm, gate_hbm, pos_hbm, out_hbm, gs_hbm, idx_v, rows_v, gate_v, lsem, ssem):
        wid = _sc_worker()
        for j in range(kk):
            pltpu.sync_copy(pos_hbm.at[j, pl.ds(wid * n_ch, n_ch)], idx_v.at[j])
        base = wid * (n_ch * ch)

        def load(c, slot):
            return pltpu.make_async_copy(src_hbm.at[pl.ds(base + c * ch, ch)], rows_v.at[slot], lsem.at[slot])

        def scatter(c, slot, j):
            return pltpu.make_async_copy(rows_v.at[slot], out_hbm.at[idx_v.at[j, c]], ssem.at[slot])

        load(0, 0).start()

        @pl.loop(0, n_ch, step=2)
        def _(c0):
            for slot in range(2):
                c = c0 + slot
                other = 1 - slot

                @pl.when(c >= 1)
                def _():
                    for j in range(kk):
                        scatter(c - 1, other, j).wait()

                @pl.when(c + 1 < n_ch)
                def _():
                    load(c + 1, other).start()

                load(c, slot).wait()
                for j in range(kk):
                    scatter(c, slot, j).start()
                for j in range(kk):
                    pltpu.sync_copy(gate_hbm.at[j, pl.ds(base + c * ch, ch)], gate_v.at[j])
                    pltpu.sync_copy(gate_v.at[j], gs_hbm.at[idx_v.at[j, c]])

        for j in range(kk):
            scatter(n_ch - 1, (n_ch - 1) % 2, j).wait()

    return scatter_kernel(src, gate_rows, pos3d)


def _sc_gather_rows(table, idx2d):
    n_idx_rows, ch = idx2d.shape
    d = table.shape[1]
    n_ch = n_idx_rows // SC_WORKERS
    assert n_idx_rows % SC_WORKERS == 0 and n_ch % 2 == 0

    @functools.partial(
        pl.kernel, mesh=_sc_mesh(), out_type=jax.ShapeDtypeStruct((n_idx_rows * ch, d), table.dtype),
        scratch_types=[pltpu.VMEM((n_ch, ch), I32), pltpu.VMEM((2, ch, d), table.dtype),
                       pltpu.SemaphoreType.DMA((2,)), pltpu.SemaphoreType.DMA((2,))],
        name="sc_gather_rows")
    def gather_kernel(table_hbm, idx_hbm, out_hbm, idx_v, rows_v, gsem, wsem):
        wid = _sc_worker()
        pltpu.sync_copy(idx_hbm.at[pl.ds(wid * n_ch, n_ch)], idx_v)
        base = wid * (n_ch * ch)

        def gather(c, slot):
            return pltpu.make_async_copy(table_hbm.at[idx_v.at[c]], rows_v.at[slot], gsem.at[slot])

        def write(c, slot):
            return pltpu.make_async_copy(rows_v.at[slot], out_hbm.at[pl.ds(base + c * ch, ch)], wsem.at[slot])

        gather(0, 0).start()

        @pl.loop(0, n_ch, step=2)
        def _(c0):
            for slot in range(2):
                c = c0 + slot
                other = 1 - slot

                @pl.when(c >= 1)
                def _():
                    write(c - 1, other).wait()

                @pl.when(c + 1 < n_ch)
                def _():
                    gather(c + 1, other).start()

                gather(c, slot).wait()
                write(c, slot).start()

        write(n_ch - 1, (n_ch - 1) % 2).wait()

    return gather_kernel(table, idx2d)


def _expert_kernel(eid_ref, nvalid_ref, first_ref, next_ref, slot_ref, xs_ref, gs_ref, wgu_hbm, bgu_ref, wdn_hbm, bdn_ref,
                   after_ref, ys_ref, wgu_st, wdn_st, wgu_bf, wdn_bf, sem_gu, sem_dn, *, layer):
    del after_ref
    i = pl.program_id(0)
    nvalid = nvalid_ref[i]
    active = nvalid > 0

    def weight_copies(expert, slot):
        return (pltpu.make_async_copy(wgu_hbm.at[layer, expert], wgu_st.at[slot], sem_gu.at[slot]),
                pltpu.make_async_copy(wdn_hbm.at[layer, expert], wdn_st.at[slot], sem_dn.at[slot]))

    @pl.when(i == 0)
    def _():
        for cp in weight_copies(eid_ref[0], slot_ref[0]):
            cp.start()

    @pl.when(first_ref[i] == 1)
    def _():
        slot = slot_ref[i]
        for cp in weight_copies(eid_ref[i], slot):
            cp.wait()

        @pl.when(next_ref[i] >= 0)
        def _():
            for cp in weight_copies(next_ref[i], 1 - slot):
                cp.start()

        def cast_rows(r, carry):
            rows = pl.ds(pl.multiple_of(r * WEIGHT_CAST_ROWS, WEIGHT_CAST_ROWS), WEIGHT_CAST_ROWS)
            wgu_bf[rows, :] = wgu_st[slot, rows, :].astype(BF16)
            wdn_bf[rows, :] = wdn_st[slot, rows, :].astype(BF16)
            return carry
        lax.fori_loop(0, wgu_bf.shape[0] // WEIGHT_CAST_ROWS, cast_rows, 0)

    def mlp_rows(n_rows):
        packed = xs_ref[0:n_rows, :]
        keep = lax.broadcasted_iota(I32, packed.shape, 0) < nvalid
        lo, hi = _unpack_bf16_pair(jnp.where(keep, packed, 0))
        xb = jnp.concatenate([lo.astype(BF16), hi.astype(BF16)], axis=1)
        hgu = _dot(xb, wgu_bf[...]) + bgu_ref[0, 0]
        f = hgu.shape[1] // 2
        a = jnp.minimum(hgu[:, 0:f], SWIGLU_LIMIT)
        g = jnp.clip(hgu[:, f:2 * f], -SWIGLU_LIMIT, SWIGLU_LIMIT)
        act = a * _sigmoid(SWIGLU_ALPHA * a) * (g + 1.0)
        gate = jnp.where(keep[:, 0:1], gs_ref[0:n_rows, 0:1], 0.0)
        y = (_dot(act.astype(BF16), wdn_bf[...]) + bdn_ref[0, 0]) * gate
        do = y.shape[1] // 2
        ys_ref[0:n_rows, :] = _pack_bf16_pair(y[:, 0:do], y[:, do:2 * do])

    full_rows = xs_ref.shape[0]
    lower = 0
    for n_rows in (full_rows // 4, full_rows // 2, full_rows):
        @pl.when((nvalid > lower) & (nvalid <= n_rows))
        def _(n_rows=n_rows):
            mlp_rows(n_rows)
            if n_rows < full_rows:
                ys_ref[n_rows:full_rows, :] = jnp.zeros((full_rows - n_rows, ys_ref.shape[1]), ys_ref.dtype)
        lower = n_rows

    @pl.when(jnp.logical_not(active))
    def _():
        ys_ref[...] = jnp.zeros_like(ys_ref)


def _expert_call(layer, tables, xs, gs, wgu, bgu, wdn, bdn, after):
    p_rows, dh = xs.shape
    bm = EXPERT_ROWS
    depth, n_e, d, f2 = wgu.shape
    dout = wdn.shape[3]
    assert f2 // 2 == d and d % WEIGHT_CAST_ROWS == 0
    bias_map = lambda i, eid, *_: (layer, eid[i], 0, 0)
    grid_spec = pltpu.PrefetchScalarGridSpec(
        num_scalar_prefetch=len(tables),
        grid=(p_rows // bm,),
        in_specs=[
            pl.BlockSpec((bm, dh), lambda i, *_: (i, 0)),
            pl.BlockSpec((bm, GATE_LANES), lambda i, *_: (i, 0)),
            pl.BlockSpec(memory_space=pl.ANY),
            pl.BlockSpec((1, 1, 1, f2), bias_map),
            pl.BlockSpec(memory_space=pl.ANY),
            pl.BlockSpec((1, 1, 1, dout), bias_map),
            pl.BlockSpec(memory_space=pl.ANY),
        ],
        out_specs=pl.BlockSpec((bm, dout // 2), lambda i, *_: (i, 0)),
        scratch_shapes=[
            pltpu.VMEM((2, d, f2), F32), pltpu.VMEM((2, f2 // 2, dout), F32),
            pltpu.VMEM((d, f2), BF16), pltpu.VMEM((f2 // 2, dout), BF16),
            pltpu.SemaphoreType.DMA((2,)), pltpu.SemaphoreType.DMA((2,)),
        ],
    )
    return pl.pallas_call(
        functools.partial(_expert_kernel, layer=layer),
        grid_spec=grid_spec,
        out_shape=jax.ShapeDtypeStruct((p_rows, dout // 2), I32),
        compiler_params=pltpu.CompilerParams(
            dimension_semantics=("arbitrary",), vmem_limit_bytes=VMEM_LIMIT),
        name="moe_experts",
    )(*tables, xs, gs, wgu, bgu.reshape(depth, n_e, 1, f2), wdn, bdn.reshape(depth, n_e, 1, dout), after)


def _final_kernel(x_ref, g2_ref, fg_ref, yg_ref, after_ref, o_ref):
    del after_ref
    out = _moe_residual(x_ref[...], g2_ref[...], yg_ref)
    o_ref[...] = _rms(out) * fg_ref[...]


def _final_call(x2d, g2, final_g, yg, after, row, n_rows):
    n_tok, d = x2d.shape
    t = COMBINE_ROWS
    return pl.pallas_call(
        _final_kernel,
        grid=(n_tok // t,),
        in_specs=[
            pl.BlockSpec((t, d), lambda i: (i, 0)),
            pl.BlockSpec((1, d), lambda i: (0, 0)),
            pl.BlockSpec((1, d), lambda i: (0, 0)),
            pl.BlockSpec((TOP_K, t, d // 2), lambda i: (0, i, 0)),
            pl.BlockSpec(memory_space=pl.ANY),
        ],
        out_specs=pl.BlockSpec((None, t, d), lambda i: (row, i, 0)),
        out_shape=jax.ShapeDtypeStruct((n_rows, n_tok, d), F32),
        input_output_aliases={4: 0} if row > 0 else {},
        compiler_params=pltpu.CompilerParams(
            dimension_semantics=("arbitrary",), vmem_limit_bytes=VMEM_LIMIT),
        name="final_residual_norm",
    )(x2d, g2, final_g, yg, after)


def _pos_kernel(pstart_ref, eidx_ref, rank_ref, pos_ref):
    e = eidx_ref[...]
    pos = rank_ref[...]
    for j in range(N_EXPERTS):
        pos = pos + jnp.where(e == j, pstart_ref[j], 0)
    pos_ref[...] = pos


def _pos_call(pstarts, eidx, rank):
    spec = pl.BlockSpec(eidx.shape, lambda i, ps: (0, 0))
    return pl.pallas_call(
        _pos_kernel,
        grid_spec=pltpu.PrefetchScalarGridSpec(num_scalar_prefetch=1, grid=(1,), in_specs=[spec, spec],
                                               out_specs=spec),
        out_shape=jax.ShapeDtypeStruct(eidx.shape, I32),
        name="moe_positions",
    )(pstarts, eidx, rank)


def _route_tables(eidx, rank, cnt, n_blocks):
    bm = EXPERT_ROWS
    counts = cnt[:, 0].astype(I32)
    padded = ((counts + bm - 1) // bm) * bm
    pends = jnp.cumsum(padded)
    pstarts = pends - padded
    pos = _pos_call(pstarts, eidx, rank)
    block_row = jnp.arange(n_blocks, dtype=I32) * bm
    block_eid = jnp.minimum(jnp.sum((pends[None, :] <= block_row[:, None]).astype(I32), axis=1), N_EXPERTS - 1)
    experts = jnp.arange(N_EXPERTS, dtype=I32)
    onehot = block_eid[:, None] == experts[None, :]
    row_end = jnp.sum(jnp.where(onehot, (pstarts + counts)[None, :], 0), axis=1)
    block_valid = jnp.clip(row_end - block_row, 0, bm).astype(I32)
    active = block_valid > 0
    prev_eid = jnp.concatenate([jnp.full((1,), -1, I32), block_eid[:-1]])
    first = (active & (block_eid != prev_eid)).astype(I32)
    later = (experts[None, :] > experts[:, None]) & (counts[None, :] > 0)
    next_expert = jnp.min(jnp.where(later, experts[None, :], N_EXPERTS), axis=1)
    next_expert = jnp.where(next_expert == N_EXPERTS, -1, next_expert)
    block_next = jnp.sum(jnp.where(onehot, next_expert[None, :], 0), axis=1).astype(I32)
    slot = ((jnp.cumsum(first) - 1) % 2).astype(I32)
    return pos, (block_eid, block_valid, first, block_next, slot)


def _moe_rows(layer, hn2p, eidx, gates, rank, cnt, wgu, bgu, wdn, bdn, after):
    n_tok, dh = hn2p.shape
    n_blocks = n_tok * TOP_K // EXPERT_ROWS + N_EXPERTS
    pos, tables = _route_tables(eidx, rank, cnt, n_blocks)
    xs, gs = _sc_scatter_rows(hn2p, gates, pos.reshape(TOP_K, n_tok // SC_CHUNK, SC_CHUNK),
                              n_blocks * EXPERT_ROWS)
    ys = _expert_call(layer, tables, xs, gs, wgu, bgu, wdn, bdn, after)
    yg = _sc_gather_rows(ys, pos.reshape(TOP_K * n_tok // SC_CHUNK, SC_CHUNK))
    return ys, yg.reshape(TOP_K, n_tok, dh)


def kernel(x, c, w_ada, b_ada, norm1_g, norm2_g, w_in, conv_w, conv_norm_g, lower_bounds, hgrn_norm_g, w_out,
           w_router, b_router, w_gu, b_gu, w_down, b_down, final_g):
    depth = w_ada.shape[0]
    bsz, seq, d = x.shape
    conv_ch = conv_w.shape[2]
    c_pad = jnp.zeros((SUBLANES, d), F32).at[:bsz].set(c)
    ada = _ada_call(c_pad, w_ada, b_ada)[:, :bsz].reshape(depth, bsz, 6, d)
    lb_all = _bounds_call(lower_bounds)

    ci = jnp.arange(conv_ch)
    gmean = jnp.where((ci[:, None] // CONV_GROUP) == (ci[None, :] // CONV_GROUP), 1.0 / CONV_GROUP, 0.0).astype(BF16)
    ti = jnp.arange(MIX_ROWS)
    upper = (ti[:, None] < ti[None, :]).astype(BF16)

    def mixer(l, b, xin, row, after, moe_input=None):
        win, wout = w_in[l].astype(BF16), w_out[l].astype(BF16)
        wr_hi = w_router[l].astype(BF16)
        wr_lo = (w_router[l] - wr_hi.astype(F32)).astype(BF16)
        wrt = jnp.zeros((d, 256), BF16).at[:, :N_EXPERTS].set(wr_hi).at[:, 128:128 + N_EXPERTS].set(wr_lo)
        br = jnp.broadcast_to(b_router[l][:, None], (N_EXPERTS, 128))
        return _mixer_call(xin, row, ada[l, b:b + 1], norm1_g[l][None], win, conv_w[l], conv_norm_g[l][None],
                           lb_all[l][None], hgrn_norm_g[l][None], wout, gmean, norm2_g[l][None], wrt, br, upper,
                           after, moe_input)

    def moe_input_of(l, b, yg):
        return (ada[l, b:b + 1, 5:6, :], yg)

    assert bsz == 2
    m0 = mixer(0, 0, x, 0, c_pad)
    m1 = mixer(0, 1, x, 1, m0[5])
    for l in range(depth):
        ys0, yg0 = _moe_rows(l, m0[1], m0[2], m0[3], m0[4], m0[5], w_gu, b_gu, w_down, b_down, m1[5])
        ys1, yg1 = _moe_rows(l, m1[1], m1[2], m1[3], m1[4], m1[5], w_gu, b_gu, w_down, b_down, ys0)
        if l + 1 < depth:
            n0 = mixer(l + 1, 0, m0[0], 0, ys1, moe_input_of(l, 0, yg0))
            n1 = mixer(l + 1, 1, m1[0], 0, n0[5], moe_input_of(l, 1, yg1))
            m0, m1 = n0, n1
    out = ys1
    for b, (mixed, yg) in enumerate(((m0, yg0), (m1, yg1))):
        g2, _ = moe_input_of(depth - 1, b, yg)
        out = _final_call(mixed[0].reshape(seq, d), g2.reshape(1, d), final_g[None], yg, out, row=b, n_rows=bsz)
    return out
```
